```python
import math
import jax, jax.numpy as jnp
from jax import lax
import numpy as np

D_MODEL = 1024
BATCH = 8
SEQ = 8192
DEPTH = 2

GRID_W = 64
CTX_LEN = 256
N_BRANCH = 4
BRANCH_W = 256
EPS = 1e-6
ROPE_BASE = 10000.0
BLOCK = 128
NEG_INF = -1e30

POOL_WINDOWS = (2, 4, 8, 16)
POOL_GROUP = BRANCH_W // len(POOL_WINDOWS)

MLA_HEADS = 4
MLA_NOPE = 64
MLA_ROPE = 32
MLA_V = 64
MLA_Q_RANK = 192
MLA_KV_RANK = 128

DIFF_HEADS = 4
DIFF_QK = 32
DIFF_V = 2 * DIFF_QK

SWA_HEADS = 4
SWA_KV_HEADS = 2
SWA_HEAD = 64
SWA_WINDOW = 128

SPLITS = (
    ('pool_in', BRANCH_W),
    ('mla_cq', MLA_Q_RANK),
    ('mla_ckv', MLA_KV_RANK),
    ('mla_kr', MLA_ROPE),
    ('diff_q', DIFF_HEADS * 2 * DIFF_QK),
    ('diff_k', DIFF_HEADS * 2 * DIFF_QK),
    ('diff_v', DIFF_HEADS * DIFF_V),
    ('swa_q', SWA_HEADS * SWA_HEAD),
    ('swa_k', SWA_KV_HEADS * SWA_HEAD),
    ('swa_v', SWA_KV_HEADS * SWA_HEAD),
    ('gates', N_BRANCH * BRANCH_W),
    ('merge', N_BRANCH * D_MODEL),
)
IN_COLS = sum(w for _, w in SPLITS)

kernel_name = 'hybrid_pool_mla_diff_swa_prefix_block'


def rms_norm(x, g):
    xf = x.astype(jnp.float32)
    y = xf * lax.rsqrt(jnp.mean(xf * xf, axis=-1, keepdims=True) + EPS)
    return (y * g.astype(jnp.float32)).astype(x.dtype)


def split_cols(z):
    out, off = {}, 0
    for name, w in SPLITS:
        out[name] = z[..., off:off + w]
        off += w
    return out


def axial_rope_tables(n_tok, rot_dim, dtype):
    t = jnp.arange(n_tok, dtype=jnp.int32)
    row = (t // GRID_W).astype(jnp.float32)
    col = (t % GRID_W).astype(jnp.float32)
    n_freq = rot_dim // 4
    inv_freq = jnp.exp(-math.log(ROPE_BASE) * jnp.arange(n_freq, dtype=jnp.float32) / n_freq)
    ang = jnp.concatenate([row[:, None] * inv_freq, col[:, None] * inv_freq], axis=-1)
    return jnp.cos(ang).astype(dtype), jnp.sin(ang).astype(dtype)


def apply_rope(x, cos, sin):
    half = x.shape[-1] // 2
    x1, x2 = x[..., :half], x[..., half:]
    c = cos[None, :, None, :]
    s = sin[None, :, None, :]
    return jnp.concatenate([x1 * c - x2 * s, x1 * s + x2 * c], axis=-1)


def pool_mixer(u, w_pool, s_pool):
    B, N, W = u.shape
    uf = u.astype(jnp.float32)
    cs = jnp.concatenate([jnp.zeros((B, 1, W), jnp.float32), jnp.cumsum(uf, axis=1)], axis=1)
    t = jnp.arange(N, dtype=jnp.int32)
    groups = []
    for gi, w in enumerate(POOL_WINDOWS):
        lo = jnp.clip(t - w // 2, 0, N)
        hi = jnp.clip(t - w // 2 + w, 0, N)
        sl = slice(gi * POOL_GROUP, (gi + 1) * POOL_GROUP)
        csg = cs[..., sl]
        win_sum = jnp.take(csg, hi, axis=1) - jnp.take(csg, lo, axis=1)
        mean = win_sum / (hi - lo).astype(jnp.float32)[None, :, None]
        groups.append(mean - uf[..., sl])
    y = jnp.stack(groups, axis=2).astype(u.dtype)
    y = jnp.einsum('bngc,gcd->bngd', y, w_pool).reshape(B, N, W)
    return y * s_pool


def dense_attention(q, k, v, coef, scale):
    M, B, S, H, D = q.shape
    nb = S // BLOCK
    qb = jnp.moveaxis(q.reshape(M, B, nb, BLOCK, H, D), 2, 0)
    cf = coef.astype(jnp.float32)

    def one(q_blk):
        s = jnp.einsum('mbqhd,mbkhd->mbhqk', q_blk, k).astype(jnp.float32) * scale
        p = jnp.einsum('m,mbhqk->bhqk', cf, jax.nn.softmax(s, axis=-1))
        return jnp.einsum('bhqk,bkhd->bqhd', p.astype(v.dtype), v)

    o = lax.map(one, qb)
    return jnp.moveaxis(o, 0, 1).reshape(B, S, H, v.shape[-1])


def sink_attend(q, keys, values, masks, sink_hg, scale):
    scores = []
    for k, m in zip(keys, masks):
        s = jnp.einsum('bqhgd,bkhd->bhgqk', q, k).astype(jnp.float32) * scale
        if m is not None:
            s = jnp.where(m, s, NEG_INF)
        scores.append(s)
    s_sink = jnp.broadcast_to(sink_hg[None, :, :, None, None], scores[0].shape[:-1] + (1,))
    p = jax.nn.softmax(jnp.concatenate(scores + [s_sink], axis=-1), axis=-1)
    outs, off = [], 0
    for s, v in zip(scores, values):
        n = s.shape[-1]
        outs.append(jnp.einsum('bhgqk,bkhd->bqhgd', p[..., off:off + n].astype(v.dtype), v))
        off += n
    return sum(outs)


def windowed_attention(q, k, v, k_ctx, v_ctx, sink_hg):
    B, S, Hkv, G, D = q.shape
    nb = S // BLOCK
    scale = D ** -0.5
    qb = jnp.moveaxis(q.reshape(B, nb, BLOCK, Hkv, G, D), 1, 0)
    pad = ((0, 0), (BLOCK, BLOCK), (0, 0), (0, 0))
    kp = jnp.pad(k, pad).reshape(B, nb + 2, BLOCK, Hkv, D)
    vp = jnp.pad(v, pad).reshape(B, nb + 2, BLOCK, Hkv, D)
    kw = jnp.moveaxis(jnp.concatenate([kp[:, :-2], kp[:, 1:-1], kp[:, 2:]], axis=2), 1, 0)
    vw = jnp.moveaxis(jnp.concatenate([vp[:, :-2], vp[:, 1:-1], vp[:, 2:]], axis=2), 1, 0)
    a = jnp.arange(BLOCK)[:, None]
    j = jnp.arange(3 * BLOCK)[None, :]
    band = jnp.abs(a + BLOCK - j) <= SWA_WINDOW
    kpos = (jnp.arange(nb)[:, None] - 1) * BLOCK + jnp.arange(3 * BLOCK)[None, :]
    inside = (kpos >= 0) & (kpos < S)
    mask = band[None] & inside[:, None, :]

    def one(args):
        qi, ki, vi, mi = args
        return sink_attend(qi, [k_ctx, ki], [v_ctx, vi], [None, mi], sink_hg, scale)

    o = lax.map(one, (qb, kw, vw, mask))
    return jnp.moveaxis(o, 0, 1).reshape(B, S, Hkv * G * D)


def hybrid_layer(layer_idx, x, xc, c, c_ctx, w_mod, b_mod, g_pre, g_post, w_in, w_pool, s_pool,
                 g_cq, w_uq, g_ckv, w_uk, w_uv, lam_q1, lam_k1, lam_q2, lam_k2, g_diff,
                 sink, w_br, w_out, need_ctx):
    B, S, D = x.shape
    C = xc.shape[1]
    G = SWA_HEADS // SWA_KV_HEADS
    f32 = jnp.float32

    shift, scale, gate = jnp.split(jax.nn.silu(c) @ w_mod + b_mod, 3, axis=-1)
    shift_c, scale_c, gate_c = jnp.split(jax.nn.silu(c_ctx) @ w_mod + b_mod, 3, axis=-1)
    h = rms_norm(x, g_pre) * (1 + scale[:, None, :]) + shift[:, None, :]
    hc = rms_norm(xc, g_pre) * (1 + scale_c) + shift_c
    z = split_cols(h @ w_in)
    zc = split_cols(hc @ w_in)

    rope_tab = {d: axial_rope_tables(S, d, x.dtype) for d in (MLA_ROPE, DIFF_QK, SWA_HEAD)}

    def rot(t, d):
        return apply_rope(t, *rope_tab[d])

    def mla_q(zz, pos):
        n = zz['mla_cq'].shape[1]
        q = (rms_norm(zz['mla_cq'], g_cq) @ w_uq).reshape(B, n, MLA_HEADS, MLA_NOPE + MLA_ROPE)
        q_rope = rot(q[..., MLA_NOPE:], MLA_ROPE) if pos else q[..., MLA_NOPE:]
        return jnp.concatenate([q[..., :MLA_NOPE], q_rope], axis=-1)

    def mla_kv(zz, pos):
        n = zz['mla_ckv'].shape[1]
        ckv = rms_norm(zz['mla_ckv'], g_ckv)
        k_nope = (ckv @ w_uk).reshape(B, n, MLA_HEADS, MLA_NOPE)
        v = (ckv @ w_uv).reshape(B, n, MLA_HEADS, MLA_V)
        k_rope = zz['mla_kr'][:, :, None, :]
        if pos:
            k_rope = rot(k_rope, MLA_ROPE)
        k = jnp.concatenate([k_nope, jnp.broadcast_to(k_rope, (B, n, MLA_HEADS, MLA_ROPE))], axis=-1)
        return k, v

    one_map = jnp.ones((1,), f32)
    mla_scale = (MLA_NOPE + MLA_ROPE) ** -0.5
    k_mc, v_mc = mla_kv(zc, False)
    k_ml, v_ml = mla_kv(z, True)
    y_mla = dense_attention(mla_q(z, True)[None], jnp.concatenate([k_mc, k_ml], axis=1)[None],
                            jnp.concatenate([v_mc, v_ml], axis=1), one_map, mla_scale).reshape(B, S, MLA_HEADS * MLA_V)

    def diff_split(t, pos):
        n = t.shape[1]
        t = t.reshape(B, n, DIFF_HEADS * 2, DIFF_QK)
        if pos:
            t = rot(t, DIFF_QK)
        return jnp.moveaxis(t.reshape(B, n, DIFF_HEADS, 2, DIFF_QK), 3, 0)

    lam_init = 0.8 - 0.6 * math.exp(-0.3 * layer_idx)
    lam = (jnp.exp(jnp.sum(lam_q1.astype(f32) * lam_k1.astype(f32)))
           - jnp.exp(jnp.sum(lam_q2.astype(f32) * lam_k2.astype(f32))) + lam_init)
    diff_coef = jnp.stack([jnp.ones((), f32), -lam])

    def diff_out(o):
        return (rms_norm(o, g_diff) * (1 - lam_init)).reshape(B, o.shape[1], DIFF_HEADS * DIFF_V)

    k_dc = diff_split(zc['diff_k'], False)
    v_dc = zc['diff_v'].reshape(B, C, DIFF_HEADS, DIFF_V)
    k_dl = diff_split(z['diff_k'], True)
    v_dl = z['diff_v'].reshape(B, S, DIFF_HEADS, DIFF_V)
    y_diff = diff_out(dense_attention(diff_split(z['diff_q'], True), jnp.concatenate([k_dc, k_dl], axis=2),
                                      jnp.concatenate([v_dc, v_dl], axis=1), diff_coef, DIFF_QK ** -0.5))

    sink_hg = sink.astype(f32).reshape(SWA_KV_HEADS, G)

    def swa_q(zz, pos):
        n = zz['swa_q'].shape[1]
        q = zz['swa_q'].reshape(B, n, SWA_HEADS, SWA_HEAD)
        if pos:
            q = rot(q, SWA_HEAD)
        return q.reshape(B, n, SWA_KV_HEADS, G, SWA_HEAD)

    def swa_kv(zz, pos):
        n = zz['swa_k'].shape[1]
        k = zz['swa_k'].reshape(B, n, SWA_KV_HEADS, SWA_HEAD)
        if pos:
            k = rot(k, SWA_HEAD)
        return k, zz['swa_v'].reshape(B, n, SWA_KV_HEADS, SWA_HEAD)

    k_sc, v_sc = swa_kv(zc, False)
    k_sl, v_sl = swa_kv(z, True)
    y_swa = windowed_attention(swa_q(z, True), k_sl, v_sl, k_sc, v_sc, sink_hg)

    y_pool = pool_mixer(z['pool_in'], w_pool, s_pool)

    def merge(ys, zz):
        n = zz['gates'].shape[1]
        gates = jax.nn.silu(zz['gates']).reshape(B, n, N_BRANCH, BRANCH_W)
        mgate = jax.nn.sigmoid(zz['merge']).reshape(B, n, N_BRANCH, D)
        m = sum(mgate[:, :, r] * ((ys[r] * gates[:, :, r]) @ w_br[r]) for r in range(N_BRANCH))
        return rms_norm(m @ w_out, g_post)

    x_new = x + gate[:, None, :] * merge([y_pool, y_mla, y_diff, y_swa], z)

    if need_ctx:
        yc_pool = pool_mixer(zc['pool_in'], w_pool, s_pool)
        yc_mla = dense_attention(mla_q(zc, False)[None], k_mc[None], v_mc, one_map,
                                 mla_scale).reshape(B, C, MLA_HEADS * MLA_V)
        yc_diff = diff_out(dense_attention(diff_split(zc['diff_q'], False), k_dc, v_dc, diff_coef, DIFF_QK ** -0.5))
        yc_swa = sink_attend(swa_q(zc, False), [k_sc], [v_sc], [None], sink_hg,
                             SWA_HEAD ** -0.5).reshape(B, C, SWA_HEADS * SWA_HEAD)
        xc = xc + gate_c * merge([yc_pool, yc_mla, yc_diff, yc_swa], zc)
    return x_new, xc


def setup_inputs(seed: int = 0) -> dict:
    key = jax.random.key(seed)
    ks = jax.random.split(key, 24)
    L = DEPTH

    def nrm(k, shape, s):
        return jax.random.normal(k, shape, jnp.float32) * s

    return {
        'x': nrm(ks[0], (BATCH, SEQ, D_MODEL), 1.0),
        'c': nrm(ks[1], (BATCH, D_MODEL), 1.0),
        'ctx': nrm(ks[2], (BATCH, CTX_LEN, D_MODEL), 1.0),
        'c_ctx': nrm(ks[3], (D_MODEL,), 1.0),
        'w_mod': nrm(ks[4], (L, D_MODEL, 3 * D_MODEL), 0.5 * D_MODEL ** -0.5),
        'b_mod': nrm(ks[5], (L, 3 * D_MODEL), 0.01),
        'g_pre': 1.0 + nrm(ks[6], (L, D_MODEL), 0.05),
        'g_post': 1.0 + nrm(ks[7], (L, D_MODEL), 0.05),
        'w_in': nrm(ks[8], (L, D_MODEL, IN_COLS), D_MODEL ** -0.5),
        'w_pool': nrm(ks[9], (L, len(POOL_WINDOWS), POOL_GROUP, POOL_GROUP), POOL_GROUP ** -0.5),
        's_pool': 1.0 + nrm(ks[10], (L, BRANCH_W), 0.05),
        'g_cq': 1.0 + nrm(ks[11], (L, MLA_Q_RANK), 0.05),
        'w_uq': nrm(ks[12], (L, MLA_Q_RANK, MLA_HEADS * (MLA_NOPE + MLA_ROPE)), MLA_Q_RANK ** -0.5),
        'g_ckv': 1.0 + nrm(ks[13], (L, MLA_KV_RANK), 0.05),
        'w_uk': nrm(ks[14], (L, MLA_KV_RANK, MLA_HEADS * MLA_NOPE), MLA_KV_RANK ** -0.5),
        'w_uv': nrm(ks[15], (L, MLA_KV_RANK, MLA_HEADS * MLA_V), MLA_KV_RANK ** -0.5),
        'lam_q1': nrm(ks[16], (L, DIFF_QK), 0.1),
        'lam_k1': nrm(ks[17], (L, DIFF_QK), 0.1),
        'lam_q2': nrm(ks[18], (L, DIFF_QK), 0.1),
        'lam_k2': nrm(ks[19], (L, DIFF_QK), 0.1),
        'g_diff': 1.0 + nrm(ks[20], (L, DIFF_V), 0.05),
        'sink': nrm(ks[21], (L, SWA_HEADS), 0.5),
        'w_br': nrm(ks[22], (L, N_BRANCH, BRANCH_W, D_MODEL), BRANCH_W ** -0.5),
        'w_out': nrm(ks[23], (L, D_MODEL, D_MODEL), D_MODEL ** -0.5),
    }


def reference(x, c, ctx, c_ctx, w_mod, b_mod, g_pre, g_post, w_in, w_pool, s_pool, g_cq, w_uq, g_ckv,
              w_uk, w_uv, lam_q1, lam_k1, lam_q2, lam_k2, g_diff, sink, w_br, w_out):
    xc = ctx
    for l in range(DEPTH):
        x, xc = hybrid_layer(l, x, xc, c, c_ctx, w_mod[l], b_mod[l], g_pre[l], g_post[l], w_in[l],
                             w_pool[l], s_pool[l], g_cq[l], w_uq[l], g_ckv[l], w_uk[l], w_uv[l],
                             lam_q1[l], lam_k1[l], lam_q2[l], lam_k2[l], g_diff[l], sink[l],
                             w_br[l], w_out[l], l < DEPTH - 1)
    return x
```

```python
import functools
import math

import jax
import jax.numpy as jnp
from jax import lax
from jax.experimental import pallas as pl
from jax.experimental.pallas import tpu as pltpu

F32 = jnp.float32
BF16 = jnp.bfloat16

GRID_W = 64
EPS = 1e-6
ROPE_BASE = 10000.0
NEG = -1e30
N_BRANCH = 4
BRANCH_W = 256
POOL_WINDOWS = (2, 4, 8, 16)
POOL_GROUP = BRANCH_W // len(POOL_WINDOWS)
POOL_HALO = 8
MLA_HEADS, MLA_NOPE, MLA_ROPE, MLA_V = 4, 64, 32, 64
MLA_Q_RANK, MLA_KV_RANK = 192, 128
DIFF_HEADS, DIFF_QK, DIFF_V = 4, 32, 64
SWA_HEADS, SWA_KV_HEADS, SWA_HEAD, SWA_WINDOW = 4, 2, 64, 128
HEAD_V = 64

LANE = 128
VMEM_LIMIT = 56 * 1024 * 1024
SEQ_TILE = 512
Q_TILE = 256

_SPLITS = (('pool_in', 256), ('mla_cq', 192), ('mla_ckv', 128), ('mla_kr', 32), ('diff_q', 256), ('diff_k', 256),
           ('diff_v', 256), ('swa_q', 256), ('swa_k', 128), ('swa_v', 128), ('gates', 1024), ('merge', 4096))
_OFF = {}
_o = 0
for _n, _w in _SPLITS:
    _OFF[_n] = (_o, _o + _w)
    _o += _w

A_POOL, A_CQ, A_CKV, A_KR, A_DQ, A_DK, A_DV, A_SQ, A_SK, A_SV, A_END = (
    0, 256, 512, 640, 768, 1024, 1280, 1536, 2048, 2176, 2304)


def _cparams(sem):
    return pltpu.CompilerParams(dimension_semantics=sem, vmem_limit_bytes=VMEM_LIMIT)


def _const_spec(shape):
    nd = len(shape)
    return pl.BlockSpec(shape, lambda *_: (0,) * nd, pipeline_mode=pl.Buffered(1))


def _rms(x, g, n):
    return x * lax.rsqrt(jnp.sum(x * x, axis=-1, keepdims=True) * (1.0 / n) + EPS) * g


def _prenorm(x, mod, g_pre):
    d = x.shape[-1]
    shift, scale = mod[:, :d], mod[:, d:2 * d]
    return _rms(x, g_pre, d) * (1.0 + scale) + shift


def _rope128(x, c, s, period, half, first_below):
    lane = lax.broadcasted_iota(jnp.int32, x.shape, 1)
    first = (lane % period) < first_below
    swapped = jnp.where(first, pltpu.roll(x, LANE - half, 1), pltpu.roll(x, half, 1))
    return x * c + swapped * s


def _rope_mla(x, c, s):
    return _rope128(x, c, s, LANE, MLA_ROPE // 2, MLA_NOPE + MLA_ROPE // 2)


def _rope_heads(x, c, s, head_dim):
    return _rope128(x, c, s, head_dim, head_dim // 2, head_dim // 2)


def _mod_kernel(c_ref, w_ref, b_ref, o_ref):
    a = c_ref[...]
    a = a * jax.nn.sigmoid(a)
    w = w_ref[0]
    a_hi = a.astype(BF16)
    a_lo = (a - a_hi.astype(F32)).astype(BF16)
    w_hi = w.astype(BF16)
    w_lo = (w - w_hi.astype(F32)).astype(BF16)
    acc = jnp.dot(a_hi, w_hi, preferred_element_type=F32)
    acc += jnp.dot(a_lo, w_hi, preferred_element_type=F32)
    acc += jnp.dot(a_hi, w_lo, preferred_element_type=F32)
    o_ref[0] = acc + b_ref[0]


def _mod_call(cc, w_mod, b_mod):
    n_layer, d, d3 = w_mod.shape
    rows = cc.shape[0]
    return pl.pallas_call(
        _mod_kernel,
        grid=(n_layer, d3 // d),
        in_specs=[pl.BlockSpec((rows, d), lambda l, j: (0, 0)),
                  pl.BlockSpec((1, d, d), lambda l, j: (l, 0, j)),
                  pl.BlockSpec((1, 1, d), lambda l, j: (l, 0, j))],
        out_specs=pl.BlockSpec((1, rows, d), lambda l, j: (l, 0, j)),
        out_shape=jax.ShapeDtypeStruct((n_layer, rows, d3), F32),
        compiler_params=_cparams(("arbitrary", "arbitrary")),
        name="mod",
    )(cc, w_mod, b_mod.reshape(n_layer, 1, d3))


def _proj_kernel(x_ref, mod_ref, gpre_ref, wa_ref, tabs_ref, gcq_ref, wuq_ref, gckv_ref, wuk_ref, wuv_ref,
                 u_ref, mq_ref, mk_ref, mvt_ref, dq_ref, dk_ref, dvt_ref, sq_ref, sk_ref, sv_ref):
    hb = _prenorm(x_ref[0], mod_ref[0], gpre_ref[...]).astype(BF16)

    def sect(lo, hi):
        return jnp.dot(hb, wa_ref[:, lo:hi], preferred_element_type=F32)

    tabs = tabs_ref[...]
    c32, s32, c64, s64, cm, sm = (tabs[:, LANE * j:LANE * (j + 1)] for j in range(6))

    u_ref[0] = sect(A_POOL, A_CQ)

    cqn = _rms(sect(A_CQ, A_CKV), gcq_ref[...], MLA_Q_RANK).astype(BF16)
    q = jnp.dot(cqn, wuq_ref[...], preferred_element_type=F32)
    mla_scale = (MLA_NOPE + MLA_ROPE) ** -0.5
    for h in range(MLA_HEADS):
        blk = _rope_mla(q[:, LANE * h:LANE * (h + 1)], cm, sm)
        mq_ref[0, :, LANE * h:LANE * (h + 1)] = (blk * mla_scale).astype(BF16)
    ckvn = _rms(sect(A_CKV, A_KR), gckv_ref[...], MLA_KV_RANK).astype(BF16)
    k_nope = jnp.dot(ckvn, wuk_ref[...], preferred_element_type=F32)
    k_rope = sect(A_KR, A_DQ)
    for h in range(MLA_HEADS):
        blk = _rope_mla(k_nope[:, LANE * h:LANE * (h + 1)] + k_rope, cm, sm)
        mk_ref[0, h] = blk.astype(BF16)
    mvt_ref[0, 0] = jnp.dot(ckvn, wuv_ref[...], preferred_element_type=F32).T.astype(BF16)

    dq = sect(A_DQ, A_DK)
    dk = sect(A_DK, A_DV)
    diff_scale = DIFF_QK ** -0.5
    for j in range(2):
        sl = slice(LANE * j, LANE * (j + 1))
        dq_ref[0, :, sl] = (_rope_heads(dq[:, sl], c32, s32, DIFF_QK) * diff_scale).astype(BF16)
        dk_ref[0, :, sl] = _rope_heads(dk[:, sl], c32, s32, DIFF_QK).astype(BF16)
    dvt_ref[0, 0] = sect(A_DV, A_SQ).T.astype(BF16)

    sq = sect(A_SQ, A_SK)
    swa_scale = SWA_HEAD ** -0.5
    for j in range(SWA_HEADS):
        sl = slice(LANE * j, LANE * (j + 1))
        sq_ref[0, :, sl] = (_rope_heads(sq[:, sl], c64, s64, SWA_HEAD) * swa_scale).astype(BF16)
    sk_ref[0] = _rope_heads(sect(A_SK, A_SV), c64, s64, SWA_HEAD).astype(BF16)
    sv_ref[0] = sect(A_SV, A_END).astype(BF16)


def _proj_call(x, mod, g_pre, wa, tabs, g_cq, w_uq, g_ckv, w_uk, w_uv, tile):
    b, n, d = x.shape
    nt = n // tile
    row = lambda w: pl.BlockSpec((1, tile, w), lambda i, bb: (bb, i, 0))
    vt = pl.BlockSpec((1, 1, 2 * LANE, tile), lambda i, bb: (bb, i, 0, 0))
    sds = jax.ShapeDtypeStruct
    return pl.pallas_call(
        _proj_kernel,
        grid=(nt, b),
        in_specs=[row(d),
                  pl.BlockSpec((1, 1, 3 * d), lambda i, bb: (bb, 0, 0)),
                  _const_spec(g_pre.shape), _const_spec(wa.shape),
                  pl.BlockSpec((tile, 6 * LANE), lambda i, bb: (i, 0)),
                  _const_spec(g_cq.shape), _const_spec(w_uq.shape), _const_spec(g_ckv.shape),
                  _const_spec(w_uk.shape), _const_spec(w_uv.shape)],
        out_specs=[row(256), row(512), pl.BlockSpec((1, MLA_HEADS, tile, LANE), lambda i, bb: (bb, 0, i, 0)), vt,
                   row(256), row(256), vt, row(512), row(128), row(128)],
        out_shape=[sds((b, n, 256), F32), sds((b, n, 512), BF16), sds((b, MLA_HEADS, n, LANE), BF16),
                   sds((b, nt, 256, tile), BF16), sds((b, n, 256), BF16), sds((b, n, 256), BF16),
                   sds((b, nt, 256, tile), BF16), sds((b, n, 512), BF16), sds((b, n, 128), BF16),
                   sds((b, n, 128), BF16)],
        compiler_params=_cparams(("arbitrary", "arbitrary")),
        name="proj",
    )(x, mod, g_pre, wa, tabs, g_cq, w_uq, g_ckv, w_uk, w_uv)


def _dense_kernel(*refs, kind, has_lat, lam_init, n_chunks, tk):
    refs = list(refs)
    q_ref = refs.pop(0)
    if has_lat:
        klat_ref, vlat_ref = refs.pop(0), refs.pop(0)
    kctx_ref, vctx_ref = refs.pop(0), refs.pop(0)
    if kind == "diff":
        lam_ref, gd_ref = refs.pop(0), refs.pop(0)
    (o_ref,) = refs
    tq = q_ref.shape[1]

    q_all = q_ref[0]
    outs = []
    for j in range(2):
        if kind == "mla":
            keys = lambda ref, rws: ref[0, j, rws, :]
            qms = [q_ref[0, :, LANE * j:LANE * (j + 1)]]
        else:
            keys = lambda ref, rws: ref[0, rws, :]
            lane = lax.broadcasted_iota(jnp.int32, q_all.shape, 1)
            qms = []
            for m in range(2):
                lo = 2 * DIFF_QK * j + DIFF_QK * m
                qms.append(jnp.where((lane >= lo) & (lane < lo + DIFF_QK), q_all, jnp.zeros_like(q_all)))

        def update(state, kc, vtc):
            new = []
            for (m_run, l_run, acc), qm in zip(state, qms):
                s = lax.dot_general(kc, qm, (((1,), (1,)), ((), ())), preferred_element_type=F32)
                m_new = jnp.maximum(m_run, jnp.max(s, axis=0, keepdims=True))
                alpha = jnp.exp(m_run - m_new)
                p = jnp.exp(s - m_new)
                l_new = alpha * l_run + jnp.sum(p, axis=0, keepdims=True)
                acc_new = alpha * acc + jnp.dot(vtc, p.astype(BF16), preferred_element_type=F32)
                new.append((m_new, l_new, acc_new))
            return tuple(new)

        rows = slice(HEAD_V * j, HEAD_V * (j + 1))
        state = tuple((jnp.full((1, tq), NEG, F32), jnp.zeros((1, tq), F32), jnp.zeros((HEAD_V, tq), F32))
                      for _ in qms)
        state = update(state, keys(kctx_ref, slice(None)), vctx_ref[0, 0, rows, :])
        if has_lat:
            def body(c, st):
                start = pl.multiple_of(c * tk, tk)
                return update(st, keys(klat_ref, pl.ds(start, tk)), vlat_ref[0, c, rows, :])
            state = lax.fori_loop(0, n_chunks, body, state)

        if kind == "mla":
            (_, l_run, acc), = state
            outs.append(acc / l_run)
        else:
            lam_v = lam_ref[...]
            lam = (jnp.exp(jnp.sum(lam_v[0:1] * lam_v[1:2], axis=1, keepdims=True))
                   - jnp.exp(jnp.sum(lam_v[2:3] * lam_v[3:4], axis=1, keepdims=True)) + lam_init)
            (_, l1, a1), (_, l2, a2) = state
            o = a1 / l1 - lam * (a2 / l2)
            o = o * lax.rsqrt(jnp.sum(o * o, axis=0, keepdims=True) * (1.0 / DIFF_V) + EPS)
            outs.append(o * gd_ref[...] * (1.0 - lam_init))
    o_ref[0] = jnp.concatenate(outs, axis=0).T.astype(BF16)


def _dense_call(kind, q, k_lat, vt_lat, k_ctx, vt_ctx, extra, lam_init, tq):
    b, n, _ = q.shape
    w = 2 * LANE if kind == "mla" else LANE
    has_lat = k_lat is not None
    c = vt_ctx.shape[3]

    def key_spec(rows):
        if kind == "mla":
            return pl.BlockSpec((1, 2, rows, LANE), lambda bb, hp, i: (bb, hp, 0, 0))
        return pl.BlockSpec((1, rows, LANE), lambda bb, hp, i: (bb, 0, hp))

    args, specs = [q], [pl.BlockSpec((1, tq, w), lambda bb, hp, i: (bb, i, hp))]
    n_chunks = tk = 0
    if has_lat:
        n_chunks, tk = vt_lat.shape[1], vt_lat.shape[3]
        args += [k_lat, vt_lat]
        specs += [key_spec(n_chunks * tk), pl.BlockSpec((1, n_chunks, LANE, tk), lambda bb, hp, i: (bb, 0, hp, 0))]
    args += [k_ctx, vt_ctx]
    specs += [key_spec(c), pl.BlockSpec((1, 1, LANE, c), lambda bb, hp, i: (bb, 0, hp, 0))]
    for e in extra:
        args.append(e)
        specs.append(pl.BlockSpec(e.shape, lambda bb, hp, i: (0, 0)))
    return pl.pallas_call(
        functools.partial(_dense_kernel, kind=kind, has_lat=has_lat, lam_init=lam_init, n_chunks=n_chunks, tk=tk),
        grid=(b, 2, n // tq),
        in_specs=specs,
        out_specs=pl.BlockSpec((1, tq, LANE), lambda bb, hp, i: (bb, i, hp)),
        out_shape=jax.ShapeDtypeStruct((b, n, 2 * LANE), BF16),
        compiler_params=_cparams(("arbitrary", "arbitrary", "arbitrary")),
        name="dense_" + kind + ("_lat" if has_lat else "_ctx"),
    )(*args)


def _swa_kernel(*refs, has_win, n_total):
    refs = list(refs)
    q_ref = refs.pop(0)
    if has_win:
        kp_ref, km_ref, kn_ref, vp_ref, vm_ref, vn_ref = (refs.pop(0) for _ in range(6))
    kc_ref, vc_ref, sink_ref, o_ref = refs
    tq = q_ref.shape[1]
    nt = (((1,), (1,)), ((), ()))
    kc, vc = kc_ref[0], vc_ref[0]
    if has_win:
        t0 = pl.program_id(1) * tq
        kw = jnp.concatenate([kp_ref[0], km_ref[0], kn_ref[0]], axis=0)
        vw = jnp.concatenate([vp_ref[0], vm_ref[0], vn_ref[0]], axis=0)
        nk = kw.shape[0]
        qpos = t0 + lax.broadcasted_iota(jnp.int32, (tq, nk), 0)
        kpos = t0 - SWA_WINDOW + lax.broadcasted_iota(jnp.int32, (tq, nk), 1)
        dist = qpos - kpos
        visible = (dist <= SWA_WINDOW) & (dist >= -SWA_WINDOW) & (kpos >= 0) & (kpos < n_total)
    sink = sink_ref[...]
    res = []
    for j in range(SWA_HEADS):
        qj = q_ref[0, :, LANE * j:LANE * (j + 1)]
        sink_j = sink[:, j:j + 1]
        s_c = lax.dot_general(qj, kc, nt, preferred_element_type=F32)
        m = jnp.maximum(jnp.max(s_c, axis=1, keepdims=True), sink_j)
        if has_win:
            s_w = jnp.where(visible, lax.dot_general(qj, kw, nt, preferred_element_type=F32), NEG)
            m = jnp.maximum(m, jnp.max(s_w, axis=1, keepdims=True))
        p_c = jnp.exp(s_c - m)
        den = jnp.sum(p_c, axis=1, keepdims=True) + jnp.exp(sink_j - m)
        num = jnp.dot(p_c.astype(BF16), vc, preferred_element_type=F32)
        if has_win:
            p_w = jnp.exp(s_w - m)
            den = den + jnp.sum(p_w, axis=1, keepdims=True)
            num = num + jnp.dot(p_w.astype(BF16), vw, preferred_element_type=F32)
        res.append(num / den)
    lane = lax.broadcasted_iota(jnp.int32, (tq, LANE), 1)
    low = lane < SWA_HEAD
    o_ref[0, :, 0:LANE] = jnp.where(low, res[0], pltpu.roll(res[1], SWA_HEAD, 1)).astype(BF16)
    o_ref[0, :, LANE:2 * LANE] = jnp.where(low, pltpu.roll(res[2], SWA_HEAD, 1), res[3]).astype(BF16)


def _swa_call(q, k, v, k_ctx, v_ctx, sink, tile):
    b, n, _ = q.shape
    has_win = k is not None
    c = k_ctx.shape[1]
    args, specs = [q], [pl.BlockSpec((1, tile, 4 * LANE), lambda bb, i: (bb, i, 0))]
    if has_win:
        r = tile // SWA_WINDOW
        last = n // SWA_WINDOW - 1
        prev = pl.BlockSpec((1, SWA_WINDOW, LANE), lambda bb, i: (bb, jnp.maximum(i * r - 1, 0), 0))
        main = pl.BlockSpec((1, tile, LANE), lambda bb, i: (bb, i, 0))
        nxt = pl.BlockSpec((1, SWA_WINDOW, LANE), lambda bb, i: (bb, jnp.minimum((i + 1) * r, last), 0))
        args += [k, k, k, v, v, v]
        specs += [prev, main, nxt, prev, main, nxt]
    args += [k_ctx, v_ctx, sink]
    specs += [pl.BlockSpec((1, c, LANE), lambda bb, i: (bb, 0, 0)),
              pl.BlockSpec((1, c, LANE), lambda bb, i: (bb, 0, 0)),
              pl.BlockSpec(sink.shape, lambda bb, i: (0, 0))]
    return pl.pallas_call(
        functools.partial(_swa_kernel, has_win=has_win, n_total=n),
        grid=(b, n // tile),
        in_specs=specs,
        out_specs=pl.BlockSpec((1, tile, 2 * LANE), lambda bb, i: (bb, i, 0)),
        out_shape=jax.ShapeDtypeStruct((b, n, 2 * LANE), BF16),
        compiler_params=_cparams(("arbitrary", "arbitrary")),
        name="swa_win" if has_win else "swa_ctx",
    )(*args)


def _pool(u, u_prev, u_next, t0, n_total):
    tile = u.shape[0]
    ext = jnp.concatenate([jnp.where(t0 > 0, u_prev, 0.0), u, jnp.where(t0 + tile < n_total, u_next, 0.0)], axis=0)
    n_ext = ext.shape[0]

    def ahead(a, k):
        return pltpu.roll(a, n_ext - k, 0) if k else a

    sums, run, w = [], ext, 1
    for win in POOL_WINDOWS:
        while w < win:
            run = run + ahead(run, w)
            w *= 2
        sums.append(ahead(run, POOL_HALO - win // 2)[:tile])
    grp = lax.broadcasted_iota(jnp.int32, (tile, BRANCH_W), 1) // POOL_GROUP
    t = t0 + lax.broadcasted_iota(jnp.int32, (tile, BRANCH_W), 0)
    win_sum, half = sums[-1], jnp.full((tile, BRANCH_W), POOL_WINDOWS[-1] // 2, jnp.int32)
    for g in range(len(POOL_WINDOWS) - 2, -1, -1):
        win_sum = jnp.where(grp == g, sums[g], win_sum)
        half = jnp.where(grp == g, POOL_WINDOWS[g] // 2, half)
    lo = jnp.clip(t - half, 0, n_total)
    hi = jnp.clip(t + half, 0, n_total)
    return win_sum / (hi - lo).astype(F32) - u


def _merge_kernel(x_ref, mod_ref, gpre_ref, gpost_ref, wg_ref, wm_ref, up_ref, u_ref, un_ref, wpool_ref, spool_ref,
                  ymla_ref, ydiff_ref, yswa_ref, wbr_ref, wout_ref, o_ref, *, n_total):
    x = x_ref[0]
    d = x.shape[-1]
    tile = x.shape[0]
    mod = mod_ref[0]
    hb = _prenorm(x, mod, gpre_ref[...]).astype(BF16)

    pooled = _pool(u_ref[0], up_ref[0], un_ref[0], pl.program_id(0) * tile, n_total)
    y_pool = jnp.dot(pooled.astype(BF16), wpool_ref[...], preferred_element_type=F32) * spool_ref[...]
    ys = [y_pool, ymla_ref[0].astype(F32), ydiff_ref[0].astype(F32), yswa_ref[0].astype(F32)]

    merged = jnp.zeros((tile, d), F32)
    for r in range(N_BRANCH):
        g = jnp.dot(hb, wg_ref[:, BRANCH_W * r:BRANCH_W * (r + 1)], preferred_element_type=F32)
        g = g * jax.nn.sigmoid(g)
        mg = jax.nn.sigmoid(jnp.dot(hb, wm_ref[:, d * r:d * (r + 1)], preferred_element_type=F32))
        merged = merged + mg * jnp.dot((ys[r] * g).astype(BF16), wbr_ref[r], preferred_element_type=F32)
    out = jnp.dot(merged.astype(BF16), wout_ref[...], preferred_element_type=F32)
    o_ref[0] = x + mod[:, 2 * d:] * _rms(out, gpost_ref[...], d)


def _merge_call(x, mod, g_pre, g_post, wg, wm, u, wpool, s_pool, y_mla, y_diff, y_swa, w_br, w_out, tile):
    b, n, d = x.shape
    r = tile // POOL_HALO
    last = n // POOL_HALO - 1
    row = lambda w: pl.BlockSpec((1, tile, w), lambda i, bb: (bb, i, 0))
    return pl.pallas_call(
        functools.partial(_merge_kernel, n_total=n),
        grid=(n // tile, b),
        in_specs=[row(d),
                  pl.BlockSpec((1, 1, 3 * d), lambda i, bb: (bb, 0, 0)),
                  _const_spec(g_pre.shape), _const_spec(g_post.shape), _const_spec(wg.shape), _const_spec(wm.shape),
                  pl.BlockSpec((1, POOL_HALO, BRANCH_W), lambda i, bb: (bb, jnp.maximum(i * r - 1, 0), 0)),
                  row(BRANCH_W),
                  pl.BlockSpec((1, POOL_HALO, BRANCH_W), lambda i, bb: (bb, jnp.minimum((i + 1) * r, last), 0)),
                  _const_spec(wpool.shape), _const_spec(s_pool.shape),
                  row(BRANCH_W), row(BRANCH_W), row(BRANCH_W),
                  _const_spec(w_br.shape), _const_spec(w_out.shape)],
        out_specs=row(d),
        out_shape=jax.ShapeDtypeStruct((b, n, d), F32),
        compiler_params=_cparams(("arbitrary", "arbitrary")),
        name="merge",
    )(x, mod, g_pre, g_post, wg, wm, u, u, u, wpool, s_pool, y_mla, y_diff, y_swa, w_br, w_out)


def _rope_tables(n_tok, identity):
    if identity:
        one, zero = jnp.ones((n_tok, LANE), F32), jnp.zeros((n_tok, LANE), F32)
        return jnp.concatenate([one, zero, one, zero, one, zero], axis=1)
    t = jnp.arange(n_tok, dtype=jnp.int32)
    row = (t // GRID_W).astype(F32)[:, None]
    col = (t % GRID_W).astype(F32)[:, None]

    def cs(rot_dim):
        n_freq = rot_dim // 4
        inv = jnp.exp(-math.log(ROPE_BASE) * jnp.arange(n_freq, dtype=F32) / n_freq)
        ang = jnp.concatenate([row * inv, col * inv], axis=-1)
        c, s = jnp.cos(ang), jnp.sin(ang)
        return jnp.concatenate([c, c], axis=1), jnp.concatenate([-s, s], axis=1)

    c32, s32 = cs(DIFF_QK)
    c64, s64 = cs(SWA_HEAD)
    one, zero = jnp.ones((n_tok, MLA_NOPE), F32), jnp.zeros((n_tok, MLA_NOPE), F32)
    cm = jnp.concatenate([one, c32, one[:, :MLA_ROPE]], axis=1)
    sm = jnp.concatenate([zero, s32, zero[:, :MLA_ROPE]], axis=1)
    return jnp.concatenate([jnp.tile(c32, (1, 4)), jnp.tile(s32, (1, 4)), jnp.tile(c64, (1, 2)),
                            jnp.tile(s64, (1, 2)), cm, sm], axis=1)


def _pack_layer(w_in, w_pool, g_cq, w_uq, w_uk, w_br, w_out):
    d = w_in.shape[0]
    col = lambda name: w_in[:, _OFF[name][0]:_OFF[name][1]]
    z = lambda w: jnp.zeros((d, w), w_in.dtype)
    swa_q = col('swa_q')
    swa_blocks = []
    for j in range(SWA_HEADS):
        qj = swa_q[:, SWA_HEAD * j:SWA_HEAD * (j + 1)]
        swa_blocks += [qj, z(SWA_HEAD)] if j // 2 == 0 else [z(SWA_HEAD), qj]
    wa = jnp.concatenate(
        [col('pool_in'), col('mla_cq'), z(256 - MLA_Q_RANK), col('mla_ckv'),
         z(MLA_NOPE), col('mla_kr'), z(LANE - MLA_NOPE - MLA_ROPE),
         col('diff_q'), col('diff_k'), col('diff_v')] + swa_blocks + [col('swa_k'), col('swa_v')], axis=1)
    hq = MLA_NOPE + MLA_ROPE
    uq = jnp.pad(w_uq.reshape(MLA_Q_RANK, MLA_HEADS, hq), ((0, 256 - MLA_Q_RANK), (0, 0), (0, LANE - hq)))
    uk = jnp.pad(w_uk.reshape(MLA_KV_RANK, MLA_HEADS, MLA_NOPE), ((0, 0), (0, 0), (0, LANE - MLA_NOPE)))
    wpool = jnp.zeros((BRANCH_W, BRANCH_W), w_pool.dtype)
    for g in range(len(POOL_WINDOWS)):
        sl = slice(POOL_GROUP * g, POOL_GROUP * (g + 1))
        wpool = wpool.at[sl, sl].set(w_pool[g])
    return dict(
        wa=wa.astype(BF16), wg=col('gates').astype(BF16), wm=col('merge').astype(BF16),
        g_cq=jnp.pad(g_cq, (0, 256 - MLA_Q_RANK)).reshape(1, 256),
        w_uq=uq.reshape(256, MLA_HEADS * LANE).astype(BF16), w_uk=uk.reshape(MLA_KV_RANK, MLA_HEADS * LANE).astype(BF16),
        wpool=wpool.astype(BF16), w_br=w_br.astype(BF16), w_out=w_out.astype(BF16))


def kernel(x, c, ctx, c_ctx, w_mod, b_mod, g_pre, g_post, w_in, w_pool, s_pool, g_cq, w_uq, g_ckv, w_uk, w_uv,
           lam_q1, lam_k1, lam_q2, lam_k2, g_diff, sink, w_br, w_out):
    b, n, d = x.shape
    n_ctx = ctx.shape[1]
    depth = w_in.shape[0]
    tile = min(SEQ_TILE, n)
    tq = min(Q_TILE, n)
    assert n % tile == 0 and n % tq == 0 and n_ctx % LANE == 0 and n_ctx <= SEQ_TILE and d % LANE == 0

    rows = -(-(b + 1) // 8) * 8
    cc = jnp.zeros((rows, d), F32).at[:b].set(c).at[b].set(c_ctx)
    mod_all = _mod_call(cc, w_mod, b_mod)
    tabs = _rope_tables(n, False)
    tabs_ctx = _rope_tables(n_ctx, True)

    xc = ctx
    for l in range(depth):
        need_ctx = l < depth - 1
        lam_init = 0.8 - 0.6 * math.exp(-0.3 * l)
        p = _pack_layer(w_in[l], w_pool[l], g_cq[l], w_uq[l], w_uk[l], w_br[l], w_out[l])
        mod = mod_all[l, :b].reshape(b, 1, 3 * d)
        mod_c = jnp.broadcast_to(mod_all[l, b].reshape(1, 1, 3 * d), (b, 1, 3 * d))
        gp, gpo = g_pre[l].reshape(1, d), g_post[l].reshape(1, d)
        gckv = g_ckv[l].reshape(1, MLA_KV_RANK)
        wuv = w_uv[l].astype(BF16)
        lam_v = jnp.stack([lam_q1[l], lam_k1[l], lam_q2[l], lam_k2[l]])
        gd = g_diff[l].reshape(DIFF_V, 1)
        sk = sink[l].reshape(1, SWA_HEADS)
        sp = s_pool[l].reshape(1, BRANCH_W)

        proj = lambda xx, mm, tt, tl: _proj_call(xx, mm, gp, p['wa'], tt, p['g_cq'], p['w_uq'], gckv, p['w_uk'],
                                                 wuv, tl)
        u, mq, mk, mvt, dq, dk, dvt, sq, skk, sv = proj(x, mod, tabs, tile)
        uc, mqc, mkc, mvtc, dqc, dkc, dvtc, sqc, skc, svc = proj(xc, mod_c, tabs_ctx, n_ctx)

        y_mla = _dense_call("mla", mq, mk, mvt, mkc, mvtc, (), lam_init, tq)
        y_diff = _dense_call("diff", dq, dk, dvt, dkc, dvtc, (lam_v, gd), lam_init, tq)
        y_swa = _swa_call(sq, skk, sv, skc, svc, sk, tile)
        merge = lambda xx, mm, uu, ya, yb, yc, tl: _merge_call(xx, mm, gp, gpo, p['wg'], p['wm'], uu, p['wpool'], sp,
                                                              ya, yb, yc, p['w_br'], p['w_out'], tl)
        x_new = merge(x, mod, u, y_mla, y_diff, y_swa, tile)
        if need_ctx:
            tqc = min(Q_TILE, n_ctx)
            yc_mla = _dense_call("mla", mqc, None, None, mkc, mvtc, (), lam_init, tqc)
            yc_diff = _dense_call("diff", dqc, None, None, dkc, dvtc, (lam_v, gd), lam_init, tqc)
            yc_swa = _swa_call(sqc, None, None, skc, svc, sk, n_ctx)
            xc = merge(xc, mod_c, uc, yc_mla, yc_diff, yc_swa, n_ctx)
        x = x_new
    return x
```

```python
import functools
import math

import jax
import jax.numpy as jnp
from jax import lax
from jax.experimental import pallas as pl
from jax.experimental.pallas import tpu as pltpu

F32 = jnp.float32
BF16 = jnp.bfloat16

GRID_W = 64
EPS = 1e-6
ROPE_BASE = 10000.0
NEG = -1e30
N_BRANCH = 4
BRANCH_W = 256
POOL_WINDOWS = (2, 4, 8, 16)
POOL_GROUP = BRANCH_W // len(POOL_WINDOWS)
POOL_HALO = 8
MLA_HEADS, MLA_NOPE, MLA_ROPE, MLA_V = 4, 64, 32, 64
MLA_Q_RANK, MLA_KV_RANK = 192, 128
DIFF_HEADS, DIFF_QK, DIFF_V = 4, 32, 64
SWA_HEADS, SWA_KV_HEADS, SWA_HEAD, SWA_WINDOW = 4, 2, 64, 128
HEAD_V = 64
LOG2E = math.log2(math.e)

LANE = 128
ONES_ROWS = 16
VMEM_LIMIT = 56 * 1024 * 1024
SEQ_TILE = 512
Q_TILE = 512

_SPLITS = (('pool_in', 256), ('mla_cq', 192), ('mla_ckv', 128), ('mla_kr', 32), ('diff_q', 256), ('diff_k', 256),
           ('diff_v', 256), ('swa_q', 256), ('swa_k', 128), ('swa_v', 128), ('gates', 1024), ('merge', 4096))
_OFF = {}
_o = 0
for _n, _w in _SPLITS:
    _OFF[_n] = (_o, _o + _w)
    _o += _w

A_POOL, A_CQ, A_CKV, A_KR, A_DQ, A_DK, A_DV, A_SQ, A_SK, A_SV, A_END = (
    0, 256, 512, 640, 768, 1024, 1280, 1536, 2048, 2176, 2304)


def _cparams(sem):
    return pltpu.CompilerParams(dimension_semantics=sem, vmem_limit_bytes=VMEM_LIMIT)


def _const_spec(shape):
    nd = len(shape)
    return pl.BlockSpec(shape, lambda *_: (0,) * nd, pipeline_mode=pl.Buffered(1))


def _rms(x, g, n):
    return x * lax.rsqrt(jnp.sum(x * x, axis=-1, keepdims=True) * (1.0 / n) + EPS) * g


def _prenorm(x, mod, g_pre):
    d = x.shape[-1]
    shift, scale = mod[:, :d], mod[:, d:2 * d]
    return _rms(x, g_pre, d) * (1.0 + scale) + shift


def _rope128(x, c, s, period, half, first_below):
    lane = lax.broadcasted_iota(jnp.int32, x.shape, 1)
    first = (lane % period) < first_below
    swapped = jnp.where(first, pltpu.roll(x, LANE - half, 1), pltpu.roll(x, half, 1))
    return x * c + swapped * s


def _rope_mla(x, c, s):
    return _rope128(x, c, s, LANE, MLA_ROPE // 2, MLA_NOPE + MLA_ROPE // 2)


def _rope_heads(x, c, s, head_dim):
    return _rope128(x, c, s, head_dim, head_dim // 2, head_dim // 2)


def _mod_kernel(c_ref, w_ref, b_ref, o_ref):
    a = c_ref[...]
    a = a * jax.nn.sigmoid(a)
    w = w_ref[0]
    a_hi = a.astype(BF16)
    a_lo = (a - a_hi.astype(F32)).astype(BF16)
    w_hi = w.astype(BF16)
    w_lo = (w - w_hi.astype(F32)).astype(BF16)
    acc = jnp.dot(a_hi, w_hi, preferred_element_type=F32)
    acc += jnp.dot(a_lo, w_hi, preferred_element_type=F32)
    acc += jnp.dot(a_hi, w_lo, preferred_element_type=F32)
    o_ref[0] = acc + b_ref[0]


def _mod_call(cc, w_mod, b_mod):
    n_layer, d, d3 = w_mod.shape
    rows = cc.shape[0]
    return pl.pallas_call(
        _mod_kernel,
        grid=(n_layer, d3 // d),
        in_specs=[pl.BlockSpec((rows, d), lambda l, j: (0, 0)),
                  pl.BlockSpec((1, d, d), lambda l, j: (l, 0, j)),
                  pl.BlockSpec((1, 1, d), lambda l, j: (l, 0, j))],
        out_specs=pl.BlockSpec((1, rows, d), lambda l, j: (l, 0, j)),
        out_shape=jax.ShapeDtypeStruct((n_layer, rows, d3), F32),
        compiler_params=_cparams(("arbitrary", "arbitrary")),
        name="mod",
    )(cc, w_mod, b_mod.reshape(n_layer, 1, d3))


def _proj_kernel(x_ref, mod_ref, gpre_ref, wa_ref, tabs_ref, gcq_ref, wuq_ref, gckv_ref, wuk_ref, wuv_ref,
                 u_ref, mq_ref, mk_ref, mvt_ref, dq_ref, dk_ref, dvt_ref, sq_ref, sk_ref, sv_ref):
    hb = _prenorm(x_ref[0], mod_ref[0], gpre_ref[...]).astype(BF16)

    def sect(lo, hi):
        return jnp.dot(hb, wa_ref[:, lo:hi], preferred_element_type=F32)

    tabs = tabs_ref[...]
    c32, s32, c64, s64, cm, sm = (tabs[:, LANE * j:LANE * (j + 1)] for j in range(6))

    u_ref[0] = sect(A_POOL, A_CQ)

    cqn = _rms(sect(A_CQ, A_CKV), gcq_ref[...], MLA_Q_RANK).astype(BF16)
    q = jnp.dot(cqn, wuq_ref[...], preferred_element_type=F32)
    mla_scale = (MLA_NOPE + MLA_ROPE) ** -0.5 * LOG2E
    for h in range(MLA_HEADS):
        blk = _rope_mla(q[:, LANE * h:LANE * (h + 1)], cm, sm)
        mq_ref[0, :, LANE * h:LANE * (h + 1)] = (blk * mla_scale).astype(BF16)
    ckvn = _rms(sect(A_CKV, A_KR), gckv_ref[...], MLA_KV_RANK).astype(BF16)
    k_nope = jnp.dot(ckvn, wuk_ref[...], preferred_element_type=F32)
    k_rope = sect(A_KR, A_DQ)
    for h in range(MLA_HEADS):
        blk = _rope_mla(k_nope[:, LANE * h:LANE * (h + 1)] + k_rope, cm, sm)
        mk_ref[0, h] = blk.astype(BF16)
    mvt_ref[0, 0] = jnp.dot(ckvn, wuv_ref[...], preferred_element_type=F32).T.astype(BF16)

    dq = sect(A_DQ, A_DK)
    dk = sect(A_DK, A_DV)
    diff_scale = DIFF_QK ** -0.5 * LOG2E
    for j in range(2):
        sl = slice(LANE * j, LANE * (j + 1))
        dq_ref[0, :, sl] = (_rope_heads(dq[:, sl], c32, s32, DIFF_QK) * diff_scale).astype(BF16)
        dk_ref[0, :, sl] = _rope_heads(dk[:, sl], c32, s32, DIFF_QK).astype(BF16)
    dvt_ref[0, 0] = sect(A_DV, A_SQ).T.astype(BF16)

    sq = sect(A_SQ, A_SK)
    swa_scale = SWA_HEAD ** -0.5
    for j in range(SWA_HEADS):
        sl = slice(LANE * j, LANE * (j + 1))
        sq_ref[0, :, sl] = (_rope_heads(sq[:, sl], c64, s64, SWA_HEAD) * swa_scale).astype(BF16)
    sk_ref[0] = _rope_heads(sect(A_SK, A_SV), c64, s64, SWA_HEAD).astype(BF16)
    sv_ref[0] = sect(A_SV, A_END).astype(BF16)


def _proj_call(x, mod, g_pre, wa, tabs, g_cq, w_uq, g_ckv, w_uk, w_uv, tile):
    b, n, d = x.shape
    nt = n // tile
    row = lambda w: pl.BlockSpec((1, tile, w), lambda i, bb: (bb, i, 0))
    vt = pl.BlockSpec((1, 1, 2 * LANE, tile), lambda i, bb: (bb, i, 0, 0))
    sds = jax.ShapeDtypeStruct
    return pl.pallas_call(
        _proj_kernel,
        grid=(nt, b),
        in_specs=[row(d),
                  pl.BlockSpec((1, 1, 3 * d), lambda i, bb: (bb, 0, 0)),
                  _const_spec(g_pre.shape), _const_spec(wa.shape),
                  pl.BlockSpec((tile, 6 * LANE), lambda i, bb: (i, 0)),
                  _const_spec(g_cq.shape), _const_spec(w_uq.shape), _const_spec(g_ckv.shape),
                  _const_spec(w_uk.shape), _const_spec(w_uv.shape)],
        out_specs=[row(256), row(512), pl.BlockSpec((1, MLA_HEADS, tile, LANE), lambda i, bb: (bb, 0, i, 0)), vt,
                   row(256), row(256), vt, row(512), row(128), row(128)],
        out_shape=[sds((b, n, 256), F32), sds((b, n, 512), BF16), sds((b, MLA_HEADS, n, LANE), BF16),
                   sds((b, nt, 256, tile), BF16), sds((b, n, 256), BF16), sds((b, n, 256), BF16),
                   sds((b, nt, 256, tile), BF16), sds((b, n, 512), BF16), sds((b, n, 128), BF16),
                   sds((b, n, 128), BF16)],
        compiler_params=_cparams(("arbitrary", "arbitrary")),
        name="proj",
    )(x, mod, g_pre, wa, tabs, g_cq, w_uq, g_ckv, w_uk, w_uv)


def _dense_kernel(*refs, kind, has_lat, lam_init, n_chunks, tk):
    refs = list(refs)
    q_ref = refs.pop(0)
    if has_lat:
        klat_ref, vlat_ref = refs.pop(0), refs.pop(0)
    kctx_ref, vctx_ref = refs.pop(0), refs.pop(0)
    if kind == "diff":
        lam_ref, gd_ref = refs.pop(0), refs.pop(0)
    (o_ref,) = refs
    tq = q_ref.shape[1]

    chains = []
    for j in range(2):
        if kind == "mla":
            chains.append((j, q_ref[0, :, LANE * j:LANE * (j + 1)]))
        else:
            q_all = q_ref[0]
            lane = lax.broadcasted_iota(jnp.int32, q_all.shape, 1)
            for m in range(2):
                lo = 2 * DIFF_QK * j + DIFF_QK * m
                chains.append((j, jnp.where((lane >= lo) & (lane < lo + DIFF_QK), q_all, jnp.zeros_like(q_all))))

    def keys(ref, j, rws):
        return ref[0, j, rws, :] if kind == "mla" else ref[0, rws, :]

    def update(state, kcs, vtcs):
        scores = [lax.dot_general(kcs[j], qm, (((1,), (1,)), ((), ())), preferred_element_type=F32)
                  for j, qm in chains]
        probs, m_news = [], []
        for (m_run, _), s in zip(state, scores):
            m_new = jnp.maximum(m_run, jnp.max(s, axis=0, keepdims=True))
            probs.append(jnp.exp2(s - m_new).astype(BF16))
            m_news.append(m_new)
        new = []
        for (m_run, acc), (j, _), p, m_new in zip(state, chains, probs, m_news):
            acc_new = jnp.exp2(m_run - m_new) * acc + jnp.dot(vtcs[j], p, preferred_element_type=F32)
            new.append((m_new, acc_new))
        return tuple(new)

    def ext(vt):
        return jnp.concatenate([vt, jnp.ones((ONES_ROWS, vt.shape[1]), BF16)], axis=0)

    def head_rows(j):
        return slice(HEAD_V * j, HEAD_V * (j + 1))

    state = tuple((jnp.full((1, tq), NEG, F32), jnp.zeros((HEAD_V + ONES_ROWS, tq), F32)) for _ in chains)
    state = update(state, [keys(kctx_ref, j, slice(None)) for j in range(2)],
                   [ext(vctx_ref[0, 0, head_rows(j), :]) for j in range(2)])
    if has_lat:
        def body(c, st):
            start = pl.multiple_of(c * tk, tk)
            return update(st, [keys(klat_ref, j, pl.ds(start, tk)) for j in range(2)],
                          [ext(vlat_ref[0, c, head_rows(j), :]) for j in range(2)])
        state = lax.fori_loop(0, n_chunks, body, state, unroll=2)

    normed = [acc[:HEAD_V] / acc[HEAD_V:HEAD_V + 1] for _, acc in state]
    if kind == "mla":
        outs = normed
    else:
        lam_v = lam_ref[...]
        lam = (jnp.exp(jnp.sum(lam_v[0:1] * lam_v[1:2], axis=1, keepdims=True))
               - jnp.exp(jnp.sum(lam_v[2:3] * lam_v[3:4], axis=1, keepdims=True)) + lam_init)
        outs = []
        for j in range(2):
            o = normed[2 * j] - lam * normed[2 * j + 1]
            o = o * lax.rsqrt(jnp.sum(o * o, axis=0, keepdims=True) * (1.0 / DIFF_V) + EPS)
            outs.append(o * gd_ref[...] * (1.0 - lam_init))
    o_ref[0] = jnp.concatenate(outs, axis=0).T.astype(BF16)


def _dense_call(kind, q, k_lat, vt_lat, k_ctx, vt_ctx, extra, lam_init, tq):
    b, n, _ = q.shape
    w = 2 * LANE if kind == "mla" else LANE
    has_lat = k_lat is not None
    c = vt_ctx.shape[3]

    def key_spec(rows):
        if kind == "mla":
            return pl.BlockSpec((1, 2, rows, LANE), lambda bb, hp, i: (bb, hp, 0, 0))
        return pl.BlockSpec((1, rows, LANE), lambda bb, hp, i: (bb, 0, hp))

    args, specs = [q], [pl.BlockSpec((1, tq, w), lambda bb, hp, i: (bb, i, hp))]
    n_chunks = tk = 0
    if has_lat:
        n_chunks, tk = vt_lat.shape[1], vt_lat.shape[3]
        args += [k_lat, vt_lat]
        specs += [key_spec(n_chunks * tk), pl.BlockSpec((1, n_chunks, LANE, tk), lambda bb, hp, i: (bb, 0, hp, 0))]
    args += [k_ctx, vt_ctx]
    specs += [key_spec(c), pl.BlockSpec((1, 1, LANE, c), lambda bb, hp, i: (bb, 0, hp, 0))]
    for e in extra:
        args.append(e)
        specs.append(pl.BlockSpec(e.shape, lambda bb, hp, i: (0, 0)))
    return pl.pallas_call(
        functools.partial(_dense_kernel, kind=kind, has_lat=has_lat, lam_init=lam_init, n_chunks=n_chunks, tk=tk),
        grid=(b, 2, n // tq),
        in_specs=specs,
        out_specs=pl.BlockSpec((1, tq, LANE), lambda bb, hp, i: (bb, i, hp)),
        out_shape=jax.ShapeDtypeStruct((b, n, 2 * LANE), BF16),
        compiler_params=_cparams(("arbitrary", "arbitrary", "arbitrary")),
        name="dense_" + kind + ("_lat" if has_lat else "_ctx"),
    )(*args)


def _swa_kernel(*refs, has_win, n_total):
    refs = list(refs)
    q_ref = refs.pop(0)
    if has_win:
        kp_ref, km_ref, kn_ref, vp_ref, vm_ref, vn_ref = (refs.pop(0) for _ in range(6))
    kc_ref, vc_ref, sink_ref, o_ref = refs
    tq = q_ref.shape[1]
    nt = (((1,), (1,)), ((), ()))
    kc, vc = kc_ref[0], vc_ref[0]
    if has_win:
        t0 = pl.program_id(1) * tq
        kw = jnp.concatenate([kp_ref[0], km_ref[0], kn_ref[0]], axis=0)
        vw = jnp.concatenate([vp_ref[0], vm_ref[0], vn_ref[0]], axis=0)
        nk = kw.shape[0]
        qpos = t0 + lax.broadcasted_iota(jnp.int32, (tq, nk), 0)
        kpos = t0 - SWA_WINDOW + lax.broadcasted_iota(jnp.int32, (tq, nk), 1)
        dist = qpos - kpos
        visible = (dist <= SWA_WINDOW) & (dist >= -SWA_WINDOW) & (kpos >= 0) & (kpos < n_total)
    sink = sink_ref[...]
    res = []
    for j in range(SWA_HEADS):
        qj = q_ref[0, :, LANE * j:LANE * (j + 1)]
        sink_j = sink[:, j:j + 1]
        s_c = lax.dot_general(qj, kc, nt, preferred_element_type=F32)
        m = jnp.maximum(jnp.max(s_c, axis=1, keepdims=True), sink_j)
        if has_win:
            s_w = jnp.where(visible, lax.dot_general(qj, kw, nt, preferred_element_type=F32), NEG)
            m = jnp.maximum(m, jnp.max(s_w, axis=1, keepdims=True))
        p_c = jnp.exp(s_c - m)
        den = jnp.sum(p_c, axis=1, keepdims=True) + jnp.exp(sink_j - m)
        num = jnp.dot(p_c.astype(BF16), vc, preferred_element_type=F32)
        if has_win:
            p_w = jnp.exp(s_w - m)
            den = den + jnp.sum(p_w, axis=1, keepdims=True)
            num = num + jnp.dot(p_w.astype(BF16), vw, preferred_element_type=F32)
        res.append(num / den)
    lane = lax.broadcasted_iota(jnp.int32, (tq, LANE), 1)
    low = lane < SWA_HEAD
    o_ref[0, :, 0:LANE] = jnp.where(low, res[0], pltpu.roll(res[1], SWA_HEAD, 1)).astype(BF16)
    o_ref[0, :, LANE:2 * LANE] = jnp.where(low, pltpu.roll(res[2], SWA_HEAD, 1), res[3]).astype(BF16)


def _swa_call(q, k, v, k_ctx, v_ctx, sink, tile):
    b, n, _ = q.shape
    has_win = k is not None
    c = k_ctx.shape[1]
    args, specs = [q], [pl.BlockSpec((1, tile, 4 * LANE), lambda bb, i: (bb, i, 0))]
    if has_win:
        r = tile // SWA_WINDOW
        last = n // SWA_WINDOW - 1
        prev = pl.BlockSpec((1, SWA_WINDOW, LANE), lambda bb, i: (bb, jnp.maximum(i * r - 1, 0), 0))
        main = pl.BlockSpec((1, tile, LANE), lambda bb, i: (bb, i, 0))
        nxt = pl.BlockSpec((1, SWA_WINDOW, LANE), lambda bb, i: (bb, jnp.minimum((i + 1) * r, last), 0))
        args += [k, k, k, v, v, v]
        specs += [prev, main, nxt, prev, main, nxt]
    args += [k_ctx, v_ctx, sink]
    specs += [pl.BlockSpec((1, c, LANE), lambda bb, i: (bb, 0, 0)),
              pl.BlockSpec((1, c, LANE), lambda bb, i: (bb, 0, 0)),
              pl.BlockSpec(sink.shape, lambda bb, i: (0, 0))]
    return pl.pallas_call(
        functools.partial(_swa_kernel, has_win=has_win, n_total=n),
        grid=(b, n // tile),
        in_specs=specs,
        out_specs=pl.BlockSpec((1, tile, 2 * LANE), lambda bb, i: (bb, i, 0)),
        out_shape=jax.ShapeDtypeStruct((b, n, 2 * LANE), BF16),
        compiler_params=_cparams(("arbitrary", "arbitrary")),
        name="swa_win" if has_win else "swa_ctx",
    )(*args)


def _pool(u, u_prev, u_next, t0, n_total):
    tile = u.shape[0]
    ext = jnp.concatenate([jnp.where(t0 > 0, u_prev, 0.0), u, jnp.where(t0 + tile < n_total, u_next, 0.0)], axis=0)
    n_ext = ext.shape[0]

    def ahead(a, k):
        return pltpu.roll(a, n_ext - k, 0) if k else a

    sums, run, w = [], ext, 1
    for win in POOL_WINDOWS:
        while w < win:
            run = run + ahead(run, w)
            w *= 2
        sums.append(ahead(run, POOL_HALO - win // 2)[:tile])
    grp = lax.broadcasted_iota(jnp.int32, (tile, BRANCH_W), 1) // POOL_GROUP
    t = t0 + lax.broadcasted_iota(jnp.int32, (tile, BRANCH_W), 0)
    win_sum, half = sums[-1], jnp.full((tile, BRANCH_W), POOL_WINDOWS[-1] // 2, jnp.int32)
    for g in range(len(POOL_WINDOWS) - 2, -1, -1):
        win_sum = jnp.where(grp == g, sums[g], win_sum)
        half = jnp.where(grp == g, POOL_WINDOWS[g] // 2, half)
    lo = jnp.clip(t - half, 0, n_total)
    hi = jnp.clip(t + half, 0, n_total)
    return win_sum / (hi - lo).astype(F32) - u


def _merge_kernel(x_ref, mod_ref, gpre_ref, gpost_ref, wg_ref, wm_ref, up_ref, u_ref, un_ref, wpool_ref, spool_ref,
                  ymla_ref, ydiff_ref, yswa_ref, wbr_ref, wout_ref, o_ref, *, n_total):
    x = x_ref[0]
    d = x.shape[-1]
    tile = x.shape[0]
    mod = mod_ref[0]
    hb = _prenorm(x, mod, gpre_ref[...]).astype(BF16)

    pooled = _pool(u_ref[0], up_ref[0], un_ref[0], pl.program_id(0) * tile, n_total)
    y_pool = jnp.dot(pooled.astype(BF16), wpool_ref[...], preferred_element_type=F32) * spool_ref[...]
    ys = [y_pool, ymla_ref[0].astype(F32), ydiff_ref[0].astype(F32), yswa_ref[0].astype(F32)]

    merged = jnp.zeros((tile, d), F32)
    for r in range(N_BRANCH):
        g = jnp.dot(hb, wg_ref[:, BRANCH_W * r:BRANCH_W * (r + 1)], preferred_element_type=F32)
        g = g * jax.nn.sigmoid(g)
        mg = jax.nn.sigmoid(jnp.dot(hb, wm_ref[:, d * r:d * (r + 1)], preferred_element_type=F32))
        merged = merged + mg * jnp.dot((ys[r] * g).astype(BF16), wbr_ref[r], preferred_element_type=F32)
    out = jnp.dot(merged.astype(BF16), wout_ref[...], preferred_element_type=F32)
    o_ref[0] = x + mod[:, 2 * d:] * _rms(out, gpost_ref[...], d)


def _merge_call(x, mod, g_pre, g_post, wg, wm, u, wpool, s_pool, y_mla, y_diff, y_swa, w_br, w_out, tile):
    b, n, d = x.shape
    r = tile // POOL_HALO
    last = n // POOL_HALO - 1
    row = lambda w: pl.BlockSpec((1, tile, w), lambda i, bb: (bb, i, 0))
    return pl.pallas_call(
        functools.partial(_merge_kernel, n_total=n),
        grid=(n // tile, b),
        in_specs=[row(d),
                  pl.BlockSpec((1, 1, 3 * d), lambda i, bb: (bb, 0, 0)),
                  _const_spec(g_pre.shape), _const_spec(g_post.shape), _const_spec(wg.shape), _const_spec(wm.shape),
                  pl.BlockSpec((1, POOL_HALO, BRANCH_W), lambda i, bb: (bb, jnp.maximum(i * r - 1, 0), 0)),
                  row(BRANCH_W),
                  pl.BlockSpec((1, POOL_HALO, BRANCH_W), lambda i, bb: (bb, jnp.minimum((i + 1) * r, last), 0)),
                  _const_spec(wpool.shape), _const_spec(s_pool.shape),
                  row(BRANCH_W), row(BRANCH_W), row(BRANCH_W),
                  _const_spec(w_br.shape), _const_spec(w_out.shape)],
        out_specs=row(d),
        out_shape=jax.ShapeDtypeStruct((b, n, d), F32),
        compiler_params=_cparams(("arbitrary", "arbitrary")),
        name="merge",
    )(x, mod, g_pre, g_post, wg, wm, u, u, u, wpool, s_pool, y_mla, y_diff, y_swa, w_br, w_out)


def _rope_tables(n_tok, identity):
    if identity:
        one, zero = jnp.ones((n_tok, LANE), F32), jnp.zeros((n_tok, LANE), F32)
        return jnp.concatenate([one, zero, one, zero, one, zero], axis=1)
    t = jnp.arange(n_tok, dtype=jnp.int32)
    row = (t // GRID_W).astype(F32)[:, None]
    col = (t % GRID_W).astype(F32)[:, None]

    def cs(rot_dim):
        n_freq = rot_dim // 4
        inv = jnp.exp(-math.log(ROPE_BASE) * jnp.arange(n_freq, dtype=F32) / n_freq)
        ang = jnp.concatenate([row * inv, col * inv], axis=-1)
        c, s = jnp.cos(ang), jnp.sin(ang)
        return jnp.concatenate([c, c], axis=1), jnp.concatenate([-s, s], axis=1)

    c32, s32 = cs(DIFF_QK)
    c64, s64 = cs(SWA_HEAD)
    one, zero = jnp.ones((n_tok, MLA_NOPE), F32), jnp.zeros((n_tok, MLA_NOPE), F32)
    cm = jnp.concatenate([one, c32, one[:, :MLA_ROPE]], axis=1)
    sm = jnp.concatenate([zero, s32, zero[:, :MLA_ROPE]], axis=1)
    return jnp.concatenate([jnp.tile(c32, (1, 4)), jnp.tile(s32, (1, 4)), jnp.tile(c64, (1, 2)),
                            jnp.tile(s64, (1, 2)), cm, sm], axis=1)


def _pack_layer(w_in, w_pool, g_cq, w_uq, w_uk, w_br, w_out):
    d = w_in.shape[0]
    col = lambda name: w_in[:, _OFF[name][0]:_OFF[name][1]]
    z = lambda w: jnp.zeros((d, w), w_in.dtype)
    swa_q = col('swa_q')
    swa_blocks = []
    for j in range(SWA_HEADS):
        qj = swa_q[:, SWA_HEAD * j:SWA_HEAD * (j + 1)]
        swa_blocks += [qj, z(SWA_HEAD)] if j // 2 == 0 else [z(SWA_HEAD), qj]
    wa = jnp.concatenate(
        [col('pool_in'), col('mla_cq'), z(256 - MLA_Q_RANK), col('mla_ckv'),
         z(MLA_NOPE), col('mla_kr'), z(LANE - MLA_NOPE - MLA_ROPE),
         col('diff_q'), col('diff_k'), col('diff_v')] + swa_blocks + [col('swa_k'), col('swa_v')], axis=1)
    hq = MLA_NOPE + MLA_ROPE
    uq = jnp.pad(w_uq.reshape(MLA_Q_RANK, MLA_HEADS, hq), ((0, 256 - MLA_Q_RANK), (0, 0), (0, LANE - hq)))
    uk = jnp.pad(w_uk.reshape(MLA_KV_RANK, MLA_HEADS, MLA_NOPE), ((0, 0), (0, 0), (0, LANE - MLA_NOPE)))
    wpool = jnp.zeros((BRANCH_W, BRANCH_W), w_pool.dtype)
    for g in range(len(POOL_WINDOWS)):
        sl = slice(POOL_GROUP * g, POOL_GROUP * (g + 1))
        wpool = wpool.at[sl, sl].set(w_pool[g])
    return dict(
        wa=wa.astype(BF16), wg=col('gates').astype(BF16), wm=col('merge').astype(BF16),
        g_cq=jnp.pad(g_cq, (0, 256 - MLA_Q_RANK)).reshape(1, 256),
        w_uq=uq.reshape(256, MLA_HEADS * LANE).astype(BF16), w_uk=uk.reshape(MLA_KV_RANK, MLA_HEADS * LANE).astype(BF16),
        wpool=wpool.astype(BF16), w_br=w_br.astype(BF16), w_out=w_out.astype(BF16))


def kernel(x, c, ctx, c_ctx, w_mod, b_mod, g_pre, g_post, w_in, w_pool, s_pool, g_cq, w_uq, g_ckv, w_uk, w_uv,
           lam_q1, lam_k1, lam_q2, lam_k2, g_diff, sink, w_br, w_out):
    b, n, d = x.shape
    n_ctx = ctx.shape[1]
    depth = w_in.shape[0]
    tile = min(SEQ_TILE, n)
    tq = min(Q_TILE, n)
    assert n % tile == 0 and n % tq == 0 and n_ctx % LANE == 0 and n_ctx <= SEQ_TILE and d % LANE == 0

    rows = -(-(b + 1) // 8) * 8
    cc = jnp.zeros((rows, d), F32).at[:b].set(c).at[b].set(c_ctx)
    mod_all = _mod_call(cc, w_mod, b_mod)
    tabs = _rope_tables(n, False)
    tabs_ctx = _rope_tables(n_ctx, True)

    xc = ctx
    for l in range(depth):
        need_ctx = l < depth - 1
        lam_init = 0.8 - 0.6 * math.exp(-0.3 * l)
        p = _pack_layer(w_in[l], w_pool[l], g_cq[l], w_uq[l], w_uk[l], w_br[l], w_out[l])
        mod = mod_all[l, :b].reshape(b, 1, 3 * d)
        mod_c = jnp.broadcast_to(mod_all[l, b].reshape(1, 1, 3 * d), (b, 1, 3 * d))
        gp, gpo = g_pre[l].reshape(1, d), g_post[l].reshape(1, d)
        gckv = g_ckv[l].reshape(1, MLA_KV_RANK)
        wuv = w_uv[l].astype(BF16)
        lam_v = jnp.stack([lam_q1[l], lam_k1[l], lam_q2[l], lam_k2[l]])
        gd = g_diff[l].reshape(DIFF_V, 1)
        sk = sink[l].reshape(1, SWA_HEADS)
        sp = s_pool[l].reshape(1, BRANCH_W)

        proj = lambda xx, mm, tt, tl: _proj_call(xx, mm, gp, p['wa'], tt, p['g_cq'], p['w_uq'], gckv, p['w_uk'],
                                                 wuv, tl)
        u, mq, mk, mvt, dq, dk, dvt, sq, skk, sv = proj(x, mod, tabs, tile)
        uc, mqc, mkc, mvtc, dqc, dkc, dvtc, sqc, skc, svc = proj(xc, mod_c, tabs_ctx, n_ctx)

        y_mla = _dense_call("mla", mq, mk, mvt, mkc, mvtc, (), lam_init, tq)
        y_diff = _dense_call("diff", dq, dk, dvt, dkc, dvtc, (lam_v, gd), lam_init, tq)
        y_swa = _swa_call(sq, skk, sv, skc, svc, sk, tile)
        merge = lambda xx, mm, uu, ya, yb, yc, tl: _merge_call(xx, mm, gp, gpo, p['wg'], p['wm'], uu, p['wpool'], sp,
                                                              ya, yb, yc, p['w_br'], p['w_out'], tl)
        x_new = merge(x, mod, u, y_mla, y_diff, y_swa, tile)
        if need_ctx:
            tqc = min(Q_TILE, n_ctx)
            yc_mla = _dense_call("mla", mqc, None, None, mkc, mvtc, (), lam_init, tqc)
            yc_diff = _dense_call("diff", dqc, None, None, dkc, dvtc, (lam_v, gd), lam_init, tqc)
            yc_swa = _swa_call(sqc, None, None, skc, svc, sk, n_ctx)
            xc = merge(xc, mod_c, uc, yc_mla, yc_diff, yc_swa, n_ctx)
        x = x_new
    return x
```

```python
import functools
import math

import jax
import jax.numpy as jnp
from jax import lax
from jax.experimental import pallas as pl
from jax.experimental.pallas import tpu as pltpu

F32 = jnp.float32
BF16 = jnp.bfloat16

GRID_W = 64
EPS = 1e-6
ROPE_BASE = 10000.0
NEG = -1e30
N_BRANCH = 4
BRANCH_W = 256
POOL_WINDOWS = (2, 4, 8, 16)
POOL_GROUP = BRANCH_W // len(POOL_WINDOWS)
POOL_HALO = 8
MLA_HEADS, MLA_NOPE, MLA_ROPE, MLA_V = 4, 64, 32, 64
MLA_Q_RANK, MLA_KV_RANK = 192, 128
DIFF_HEADS, DIFF_QK, DIFF_V = 4, 32, 64
SWA_HEADS, SWA_KV_HEADS, SWA_HEAD, SWA_WINDOW = 4, 2, 64, 128
HEAD_V = 64
LOG2E = math.log2(math.e)

LANE = 128
ONES_ROWS = 16
VMEM_LIMIT = 56 * 1024 * 1024
SEQ_TILE = 512
Q_TILE = {"mla": 512, "diff": 256}

_SPLITS = (('pool_in', 256), ('mla_cq', 192), ('mla_ckv', 128), ('mla_kr', 32), ('diff_q', 256), ('diff_k', 256),
           ('diff_v', 256), ('swa_q', 256), ('swa_k', 128), ('swa_v', 128), ('gates', 1024), ('merge', 4096))
_OFF = {}
_o = 0
for _n, _w in _SPLITS:
    _OFF[_n] = (_o, _o + _w)
    _o += _w

A_POOL, A_CQ, A_CKV, A_KR, A_DQ, A_DK, A_DV, A_SQ, A_SK, A_SV, A_END = (
    0, 256, 512, 640, 768, 1024, 1280, 1536, 2048, 2176, 2304)


def _cparams(sem):
    return pltpu.CompilerParams(dimension_semantics=sem, vmem_limit_bytes=VMEM_LIMIT)


def _const_spec(shape):
    nd = len(shape)
    return pl.BlockSpec(shape, lambda *_: (0,) * nd, pipeline_mode=pl.Buffered(1))


def _rms(x, g, n):
    return x * lax.rsqrt(jnp.sum(x * x, axis=-1, keepdims=True) * (1.0 / n) + EPS) * g


def _prenorm(x, mod, g_pre):
    d = x.shape[-1]
    shift, scale = mod[:, :d], mod[:, d:2 * d]
    return _rms(x, g_pre, d) * (1.0 + scale) + shift


def _rope128(x, c, s, period, half, first_below):
    lane = lax.broadcasted_iota(jnp.int32, x.shape, 1)
    first = (lane % period) < first_below
    swapped = jnp.where(first, pltpu.roll(x, LANE - half, 1), pltpu.roll(x, half, 1))
    return x * c + swapped * s


def _rope_mla(x, c, s):
    return _rope128(x, c, s, LANE, MLA_ROPE // 2, MLA_NOPE + MLA_ROPE // 2)


def _rope_heads(x, c, s, head_dim):
    return _rope128(x, c, s, head_dim, head_dim // 2, head_dim // 2)


def _mod_kernel(c_ref, w_ref, b_ref, o_ref):
    a = c_ref[...]
    a = a * jax.nn.sigmoid(a)
    w = w_ref[0]
    a_hi = a.astype(BF16)
    a_lo = (a - a_hi.astype(F32)).astype(BF16)
    w_hi = w.astype(BF16)
    w_lo = (w - w_hi.astype(F32)).astype(BF16)
    acc = jnp.dot(a_hi, w_hi, preferred_element_type=F32)
    acc += jnp.dot(a_lo, w_hi, preferred_element_type=F32)
    acc += jnp.dot(a_hi, w_lo, preferred_element_type=F32)
    o_ref[0] = acc + b_ref[0]


def _mod_call(cc, w_mod, b_mod):
    n_layer, d, d3 = w_mod.shape
    rows = cc.shape[0]
    return pl.pallas_call(
        _mod_kernel,
        grid=(n_layer, d3 // d),
        in_specs=[pl.BlockSpec((rows, d), lambda l, j: (0, 0)),
                  pl.BlockSpec((1, d, d), lambda l, j: (l, 0, j)),
                  pl.BlockSpec((1, 1, d), lambda l, j: (l, 0, j))],
        out_specs=pl.BlockSpec((1, rows, d), lambda l, j: (l, 0, j)),
        out_shape=jax.ShapeDtypeStruct((n_layer, rows, d3), F32),
        compiler_params=_cparams(("arbitrary", "arbitrary")),
        name="mod",
    )(cc, w_mod, b_mod.reshape(n_layer, 1, d3))


def _proj_kernel(x_ref, mod_ref, gpre_ref, wa_ref, tabs_ref, gcq_ref, wuq_ref, gckv_ref, wuk_ref, wuv_ref,
                 u_ref, mq_ref, mk_ref, mvt_ref, dq_ref, dk_ref, dvt_ref, sq_ref, sk_ref, sv_ref):
    hb = _prenorm(x_ref[0], mod_ref[0], gpre_ref[...]).astype(BF16)

    def sect(lo, hi):
        return jnp.dot(hb, wa_ref[:, lo:hi], preferred_element_type=F32)

    tabs = tabs_ref[...]
    c32, s32, c64, s64, cm, sm = (tabs[:, LANE * j:LANE * (j + 1)] for j in range(6))

    u_ref[0] = sect(A_POOL, A_CQ)

    cqn = _rms(sect(A_CQ, A_CKV), gcq_ref[...], MLA_Q_RANK).astype(BF16)
    q = jnp.dot(cqn, wuq_ref[...], preferred_element_type=F32)
    mla_scale = (MLA_NOPE + MLA_ROPE) ** -0.5 * LOG2E
    for h in range(MLA_HEADS):
        blk = _rope_mla(q[:, LANE * h:LANE * (h + 1)], cm, sm)
        mq_ref[0, :, LANE * h:LANE * (h + 1)] = (blk * mla_scale).astype(BF16)
    ckvn = _rms(sect(A_CKV, A_KR), gckv_ref[...], MLA_KV_RANK).astype(BF16)
    k_nope = jnp.dot(ckvn, wuk_ref[...], preferred_element_type=F32)
    k_rope = sect(A_KR, A_DQ)
    for h in range(MLA_HEADS):
        blk = _rope_mla(k_nope[:, LANE * h:LANE * (h + 1)] + k_rope, cm, sm)
        mk_ref[0, h] = blk.astype(BF16)
    mvt_ref[0, 0] = jnp.dot(ckvn, wuv_ref[...], preferred_element_type=F32).T.astype(BF16)

    dq = sect(A_DQ, A_DK)
    dk = sect(A_DK, A_DV)
    diff_scale = DIFF_QK ** -0.5 * LOG2E
    for j in range(2):
        sl = slice(LANE * j, LANE * (j + 1))
        dq_ref[0, :, sl] = (_rope_heads(dq[:, sl], c32, s32, DIFF_QK) * diff_scale).astype(BF16)
        dk_ref[0, :, sl] = _rope_heads(dk[:, sl], c32, s32, DIFF_QK).astype(BF16)
    dvt_ref[0, 0] = sect(A_DV, A_SQ).T.astype(BF16)

    sq = sect(A_SQ, A_SK)
    swa_scale = SWA_HEAD ** -0.5
    for j in range(SWA_HEADS):
        sl = slice(LANE * j, LANE * (j + 1))
        sq_ref[0, :, sl] = (_rope_heads(sq[:, sl], c64, s64, SWA_HEAD) * swa_scale).astype(BF16)
    sk_ref[0] = _rope_heads(sect(A_SK, A_SV), c64, s64, SWA_HEAD).astype(BF16)
    sv_ref[0] = sect(A_SV, A_END).astype(BF16)


def _proj_call(x, mod, g_pre, wa, tabs, g_cq, w_uq, g_ckv, w_uk, w_uv, tile):
    b, n, d = x.shape
    nt = n // tile
    row = lambda w: pl.BlockSpec((1, tile, w), lambda i, bb: (bb, i, 0))
    vt = pl.BlockSpec((1, 1, 2 * LANE, tile), lambda i, bb: (bb, i, 0, 0))
    sds = jax.ShapeDtypeStruct
    return pl.pallas_call(
        _proj_kernel,
        grid=(nt, b),
        in_specs=[row(d),
                  pl.BlockSpec((1, 1, 3 * d), lambda i, bb: (bb, 0, 0)),
                  _const_spec(g_pre.shape), _const_spec(wa.shape),
                  pl.BlockSpec((tile, 6 * LANE), lambda i, bb: (i, 0)),
                  _const_spec(g_cq.shape), _const_spec(w_uq.shape), _const_spec(g_ckv.shape),
                  _const_spec(w_uk.shape), _const_spec(w_uv.shape)],
        out_specs=[row(256), row(512), pl.BlockSpec((1, MLA_HEADS, tile, LANE), lambda i, bb: (bb, 0, i, 0)), vt,
                   row(256), row(256), vt, row(512), row(128), row(128)],
        out_shape=[sds((b, n, 256), F32), sds((b, n, 512), BF16), sds((b, MLA_HEADS, n, LANE), BF16),
                   sds((b, nt, 256, tile), BF16), sds((b, n, 256), BF16), sds((b, n, 256), BF16),
                   sds((b, nt, 256, tile), BF16), sds((b, n, 512), BF16), sds((b, n, 128), BF16),
                   sds((b, n, 128), BF16)],
        compiler_params=_cparams(("arbitrary", "arbitrary")),
        name="proj",
    )(x, mod, g_pre, wa, tabs, g_cq, w_uq, g_ckv, w_uk, w_uv)


def _dense_kernel(*refs, kind, has_lat, lam_init, n_chunks, tk):
    refs = list(refs)
    q_ref = refs.pop(0)
    if has_lat:
        klat_ref, vlat_ref = refs.pop(0), refs.pop(0)
    kctx_ref, vctx_ref = refs.pop(0), refs.pop(0)
    if kind == "diff":
        lam_ref, gd_ref = refs.pop(0), refs.pop(0)
    o_ref = refs.pop(0)
    tq = q_ref.shape[1]

    chains = []
    for j in range(2):
        if kind == "mla":
            chains.append((j, q_ref[0, :, LANE * j:LANE * (j + 1)]))
        else:
            q_all = q_ref[0]
            lane = lax.broadcasted_iota(jnp.int32, q_all.shape, 1)
            for m in range(2):
                lo = 2 * DIFF_QK * j + DIFF_QK * m
                chains.append((j, jnp.where((lane >= lo) & (lane < lo + DIFF_QK), q_all, jnp.zeros_like(q_all))))

    def keys(ref, j, rws):
        return ref[0, j, rws, :] if kind == "mla" else ref[0, rws, :]

    def score(kcs):
        ss = [lax.dot_general(kcs[j], qm, (((1,), (1,)), ((), ())), preferred_element_type=F32) for j, qm in chains]
        return ss, [jnp.max(s, axis=0, keepdims=True) for s in ss]

    def absorb(state, ss, maxes, vtcs):
        new = []
        for (m_run, acc), (j, _), s, mx in zip(state, chains, ss, maxes):
            m_new = jnp.maximum(m_run, mx)
            p = jnp.exp2(s - m_new).astype(BF16)
            new.append((m_new, jnp.exp2(m_run - m_new) * acc + jnp.dot(vtcs[j], p, preferred_element_type=F32)))
        return tuple(new)

    def ext(vt):
        return jnp.concatenate([vt, jnp.ones((ONES_ROWS, vt.shape[1]), BF16)], axis=0)

    def lat_keys(c):
        start = pl.multiple_of(c * tk, tk)
        return [keys(klat_ref, j, pl.ds(start, tk)) for j in range(2)]

    def lat_vals(c):
        return [ext(vlat_ref[0, c, HEAD_V * j:HEAD_V * (j + 1), :]) for j in range(2)]

    state = tuple((jnp.full((1, tq), NEG, F32), jnp.zeros((HEAD_V + ONES_ROWS, tq), F32)) for _ in chains)
    ss_ctx, mx_ctx = score([keys(kctx_ref, j, slice(None)) for j in range(2)])
    vts_ctx = [ext(vctx_ref[0, 0, HEAD_V * j:HEAD_V * (j + 1), :]) for j in range(2)]
    if not has_lat:
        state = absorb(state, ss_ctx, mx_ctx, vts_ctx)
    else:
        (s_a, s_b), (p_a, p_b) = refs[0:2], refs[2:4]

        def park(s_buf, c):
            ss, maxes = score(lat_keys(c))
            for ch, s in enumerate(ss):
                s_buf[ch] = s
            return tuple(maxes)

        def soften(s_buf, p_buf, ms, maxes):
            new_ms, alphas = [], []
            for ch, (m_run, mx) in enumerate(zip(ms, maxes)):
                m_new = jnp.maximum(m_run, mx)
                p_buf[ch] = jnp.exp2(s_buf[ch] - m_new).astype(BF16)
                new_ms.append(m_new)
                alphas.append(jnp.exp2(m_run - m_new))
            return tuple(new_ms), tuple(alphas)

        def weigh(accs, alphas, p_buf, c):
            vts = lat_vals(c)
            return tuple(a * acc + jnp.dot(vts[j], p_buf[ch], preferred_element_type=F32)
                         for ch, (acc, a, (j, _)) in enumerate(zip(accs, alphas, chains)))

        mx0 = park(s_a, 0)
        mx_cur = park(s_b, 1)
        state = absorb(state, ss_ctx, mx_ctx, vts_ctx)
        ms, accs = tuple(m for m, _ in state), tuple(a for _, a in state)
        ms, al_prev = soften(s_a, p_a, ms, mx0)

        def body(i, carry):
            ms, accs, al_prev, mx_cur = carry
            c = 2 * i + 1
            mx_nxt = park(s_a, c + 1)
            ms, al_cur = soften(s_b, p_b, ms, mx_cur)
            accs = weigh(accs, al_prev, p_a, c - 1)
            mx_cur = park(s_b, c + 2)
            ms, al_prev = soften(s_a, p_a, ms, mx_nxt)
            accs = weigh(accs, al_cur, p_b, c)
            return ms, accs, al_prev, mx_cur

        ms, accs, al_prev, mx_cur = lax.fori_loop(0, n_chunks // 2 - 1, body, (ms, accs, al_prev, mx_cur))
        ms, al_cur = soften(s_b, p_b, ms, mx_cur)
        accs = weigh(accs, al_prev, p_a, n_chunks - 2)
        accs = weigh(accs, al_cur, p_b, n_chunks - 1)
        state = tuple(zip(ms, accs))

    normed = [acc[:HEAD_V] / acc[HEAD_V:HEAD_V + 1] for _, acc in state]
    if kind == "mla":
        outs = normed
    else:
        lam_v = lam_ref[...]
        lam = (jnp.exp(jnp.sum(lam_v[0:1] * lam_v[1:2], axis=1, keepdims=True))
               - jnp.exp(jnp.sum(lam_v[2:3] * lam_v[3:4], axis=1, keepdims=True)) + lam_init)
        outs = []
        for j in range(2):
            o = normed[2 * j] - lam * normed[2 * j + 1]
            o = o * lax.rsqrt(jnp.sum(o * o, axis=0, keepdims=True) * (1.0 / DIFF_V) + EPS)
            outs.append(o * gd_ref[...] * (1.0 - lam_init))
    o_ref[0] = jnp.concatenate(outs, axis=0).T.astype(BF16)


def _dense_call(kind, q, k_lat, vt_lat, k_ctx, vt_ctx, extra, lam_init):
    b, n, _ = q.shape
    tq = min(Q_TILE[kind], n)
    assert n % tq == 0
    w = 2 * LANE if kind == "mla" else LANE
    has_lat = k_lat is not None
    c = vt_ctx.shape[3]

    def key_spec(rows):
        if kind == "mla":
            return pl.BlockSpec((1, 2, rows, LANE), lambda bb, hp, i: (bb, hp, 0, 0))
        return pl.BlockSpec((1, rows, LANE), lambda bb, hp, i: (bb, 0, hp))

    args, specs = [q], [pl.BlockSpec((1, tq, w), lambda bb, hp, i: (bb, i, hp))]
    n_chunks = tk = 0
    scratch = []
    if has_lat:
        n_chunks, tk = vt_lat.shape[1], vt_lat.shape[3]
        assert n_chunks % 2 == 0
        n_chains = 2 if kind == "mla" else 4
        scratch = [pltpu.VMEM((n_chains, tk, tq), F32)] * 2 + [pltpu.VMEM((n_chains, tk, tq), BF16)] * 2
        args += [k_lat, vt_lat]
        specs += [key_spec(n_chunks * tk), pl.BlockSpec((1, n_chunks, LANE, tk), lambda bb, hp, i: (bb, 0, hp, 0))]
    args += [k_ctx, vt_ctx]
    specs += [key_spec(c), pl.BlockSpec((1, 1, LANE, c), lambda bb, hp, i: (bb, 0, hp, 0))]
    for e in extra:
        args.append(e)
        specs.append(pl.BlockSpec(e.shape, lambda bb, hp, i: (0, 0)))
    return pl.pallas_call(
        functools.partial(_dense_kernel, kind=kind, has_lat=has_lat, lam_init=lam_init, n_chunks=n_chunks, tk=tk),
        grid=(b, 2, n // tq),
        in_specs=specs,
        out_specs=pl.BlockSpec((1, tq, LANE), lambda bb, hp, i: (bb, i, hp)),
        out_shape=jax.ShapeDtypeStruct((b, n, 2 * LANE), BF16),
        scratch_shapes=scratch,
        compiler_params=_cparams(("arbitrary", "arbitrary", "arbitrary")),
        name="dense_" + kind + ("_lat" if has_lat else "_ctx"),
    )(*args)


def _swa_kernel(*refs, has_win, n_total):
    refs = list(refs)
    q_ref = refs.pop(0)
    if has_win:
        kp_ref, km_ref, kn_ref, vp_ref, vm_ref, vn_ref = (refs.pop(0) for _ in range(6))
    kc_ref, vc_ref, sink_ref, o_ref = refs
    tq = q_ref.shape[1]
    nt = (((1,), (1,)), ((), ()))
    kc, vc = kc_ref[0], vc_ref[0]
    if has_win:
        t0 = pl.program_id(1) * tq
        kw = jnp.concatenate([kp_ref[0], km_ref[0], kn_ref[0]], axis=0)
        vw = jnp.concatenate([vp_ref[0], vm_ref[0], vn_ref[0]], axis=0)
        nk = kw.shape[0]
        qpos = t0 + lax.broadcasted_iota(jnp.int32, (tq, nk), 0)
        kpos = t0 - SWA_WINDOW + lax.broadcasted_iota(jnp.int32, (tq, nk), 1)
        dist = qpos - kpos
        visible = (dist <= SWA_WINDOW) & (dist >= -SWA_WINDOW) & (kpos >= 0) & (kpos < n_total)
    sink = sink_ref[...]
    res = []
    for j in range(SWA_HEADS):
        qj = q_ref[0, :, LANE * j:LANE * (j + 1)]
        sink_j = sink[:, j:j + 1]
        s_c = lax.dot_general(qj, kc, nt, preferred_element_type=F32)
        m = jnp.maximum(jnp.max(s_c, axis=1, keepdims=True), sink_j)
        if has_win:
            s_w = jnp.where(visible, lax.dot_general(qj, kw, nt, preferred_element_type=F32), NEG)
            m = jnp.maximum(m, jnp.max(s_w, axis=1, keepdims=True))
        p_c = jnp.exp(s_c - m)
        den = jnp.sum(p_c, axis=1, keepdims=True) + jnp.exp(sink_j - m)
        num = jnp.dot(p_c.astype(BF16), vc, preferred_element_type=F32)
        if has_win:
            p_w = jnp.exp(s_w - m)
            den = den + jnp.sum(p_w, axis=1, keepdims=True)
            num = num + jnp.dot(p_w.astype(BF16), vw, preferred_element_type=F32)
        res.append(num / den)
    lane = lax.broadcasted_iota(jnp.int32, (tq, LANE), 1)
    low = lane < SWA_HEAD
    o_ref[0, :, 0:LANE] = jnp.where(low, res[0], pltpu.roll(res[1], SWA_HEAD, 1)).astype(BF16)
    o_ref[0, :, LANE:2 * LANE] = jnp.where(low, pltpu.roll(res[2], SWA_HEAD, 1), res[3]).astype(BF16)


def _swa_call(q, k, v, k_ctx, v_ctx, sink, tile):
    b, n, _ = q.shape
    has_win = k is not None
    c = k_ctx.shape[1]
    args, specs = [q], [pl.BlockSpec((1, tile, 4 * LANE), lambda bb, i: (bb, i, 0))]
    if has_win:
        r = tile // SWA_WINDOW
        last = n // SWA_WINDOW - 1
        prev = pl.BlockSpec((1, SWA_WINDOW, LANE), lambda bb, i: (bb, jnp.maximum(i * r - 1, 0), 0))
        main = pl.BlockSpec((1, tile, LANE), lambda bb, i: (bb, i, 0))
        nxt = pl.BlockSpec((1, SWA_WINDOW, LANE), lambda bb, i: (bb, jnp.minimum((i + 1) * r, last), 0))
        args += [k, k, k, v, v, v]
        specs += [prev, main, nxt, prev, main, nxt]
    args += [k_ctx, v_ctx, sink]
    specs += [pl.BlockSpec((1, c, LANE), lambda bb, i: (bb, 0, 0)),
              pl.BlockSpec((1, c, LANE), lambda bb, i: (bb, 0, 0)),
              pl.BlockSpec(sink.shape, lambda bb, i: (0, 0))]
    return pl.pallas_call(
        functools.partial(_swa_kernel, has_win=has_win, n_total=n),
        grid=(b, n // tile),
        in_specs=specs,
        out_specs=pl.BlockSpec((1, tile, 2 * LANE), lambda bb, i: (bb, i, 0)),
        out_shape=jax.ShapeDtypeStruct((b, n, 2 * LANE), BF16),
        compiler_params=_cparams(("arbitrary", "arbitrary")),
        name="swa_win" if has_win else "swa_ctx",
    )(*args)


def _pool(u, u_prev, u_next, t0, n_total):
    tile = u.shape[0]
    ext = jnp.concatenate([jnp.where(t0 > 0, u_prev, 0.0), u, jnp.where(t0 + tile < n_total, u_next, 0.0)], axis=0)
    n_ext = ext.shape[0]

    def ahead(a, k):
        return pltpu.roll(a, n_ext - k, 0) if k else a

    sums, run, w = [], ext, 1
    for win in POOL_WINDOWS:
        while w < win:
            run = run + ahead(run, w)
            w *= 2
        sums.append(ahead(run, POOL_HALO - win // 2)[:tile])
    grp = lax.broadcasted_iota(jnp.int32, (tile, BRANCH_W), 1) // POOL_GROUP
    t = t0 + lax.broadcasted_iota(jnp.int32, (tile, BRANCH_W), 0)
    win_sum, half = sums[-1], jnp.full((tile, BRANCH_W), POOL_WINDOWS[-1] // 2, jnp.int32)
    for g in range(len(POOL_WINDOWS) - 2, -1, -1):
        win_sum = jnp.where(grp == g, sums[g], win_sum)
        half = jnp.where(grp == g, POOL_WINDOWS[g] // 2, half)
    lo = jnp.clip(t - half, 0, n_total)
    hi = jnp.clip(t + half, 0, n_total)
    return win_sum / (hi - lo).astype(F32) - u


def _merge_kernel(x_ref, mod_ref, gpre_ref, gpost_ref, wg_ref, wm_ref, up_ref, u_ref, un_ref, wpool_ref, spool_ref,
                  ymla_ref, ydiff_ref, yswa_ref, wbr_ref, wout_ref, o_ref, *, n_total):
    x = x_ref[0]
    d = x.shape[-1]
    tile = x.shape[0]
    mod = mod_ref[0]
    hb = _prenorm(x, mod, gpre_ref[...]).astype(BF16)

    pooled = _pool(u_ref[0], up_ref[0], un_ref[0], pl.program_id(0) * tile, n_total)
    y_pool = jnp.dot(pooled.astype(BF16), wpool_ref[...], preferred_element_type=F32) * spool_ref[...]
    ys = [y_pool, ymla_ref[0].astype(F32), ydiff_ref[0].astype(F32), yswa_ref[0].astype(F32)]

    merged = jnp.zeros((tile, d), F32)
    for r in range(N_BRANCH):
        g = jnp.dot(hb, wg_ref[:, BRANCH_W * r:BRANCH_W * (r + 1)], preferred_element_type=F32)
        g = g * jax.nn.sigmoid(g)
        mg = jax.nn.sigmoid(jnp.dot(hb, wm_ref[:, d * r:d * (r + 1)], preferred_element_type=F32))
        merged = merged + mg * jnp.dot((ys[r] * g).astype(BF16), wbr_ref[r], preferred_element_type=F32)
    out = jnp.dot(merged.astype(BF16), wout_ref[...], preferred_element_type=F32)
    o_ref[0] = x + mod[:, 2 * d:] * _rms(out, gpost_ref[...], d)


def _merge_call(x, mod, g_pre, g_post, wg, wm, u, wpool, s_pool, y_mla, y_diff, y_swa, w_br, w_out, tile):
    b, n, d = x.shape
    r = tile // POOL_HALO
    last = n // POOL_HALO - 1
    row = lambda w: pl.BlockSpec((1, tile, w), lambda i, bb: (bb, i, 0))
    return pl.pallas_call(
        functools.partial(_merge_kernel, n_total=n),
        grid=(n // tile, b),
        in_specs=[row(d),
                  pl.BlockSpec((1, 1, 3 * d), lambda i, bb: (bb, 0, 0)),
                  _const_spec(g_pre.shape), _const_spec(g_post.shape), _const_spec(wg.shape), _const_spec(wm.shape),
                  pl.BlockSpec((1, POOL_HALO, BRANCH_W), lambda i, bb: (bb, jnp.maximum(i * r - 1, 0), 0)),
                  row(BRANCH_W),
                  pl.BlockSpec((1, POOL_HALO, BRANCH_W), lambda i, bb: (bb, jnp.minimum((i + 1) * r, last), 0)),
                  _const_spec(wpool.shape), _const_spec(s_pool.shape),
                  row(BRANCH_W), row(BRANCH_W), row(BRANCH_W),
                  _const_spec(w_br.shape), _const_spec(w_out.shape)],
        out_specs=row(d),
        out_shape=jax.ShapeDtypeStruct((b, n, d), F32),
        compiler_params=_cparams(("arbitrary", "arbitrary")),
        name="merge",
    )(x, mod, g_pre, g_post, wg, wm, u, u, u, wpool, s_pool, y_mla, y_diff, y_swa, w_br, w_out)


def _rope_tables(n_tok, identity):
    if identity:
        one, zero = jnp.ones((n_tok, LANE), F32), jnp.zeros((n_tok, LANE), F32)
        return jnp.concatenate([one, zero, one, zero, one, zero], axis=1)
    t = jnp.arange(n_tok, dtype=jnp.int32)
    row = (t // GRID_W).astype(F32)[:, None]
    col = (t % GRID_W).astype(F32)[:, None]

    def cs(rot_dim):
        n_freq = rot_dim // 4
        inv = jnp.exp(-math.log(ROPE_BASE) * jnp.arange(n_freq, dtype=F32) / n_freq)
        ang = jnp.concatenate([row * inv, col * inv], axis=-1)
        c, s = jnp.cos(ang), jnp.sin(ang)
        return jnp.concatenate([c, c], axis=1), jnp.concatenate([-s, s], axis=1)

    c32, s32 = cs(DIFF_QK)
    c64, s64 = cs(SWA_HEAD)
    one, zero = jnp.ones((n_tok, MLA_NOPE), F32), jnp.zeros((n_tok, MLA_NOPE), F32)
    cm = jnp.concatenate([one, c32, one[:, :MLA_ROPE]], axis=1)
    sm = jnp.concatenate([zero, s32, zero[:, :MLA_ROPE]], axis=1)
    return jnp.concatenate([jnp.tile(c32, (1, 4)), jnp.tile(s32, (1, 4)), jnp.tile(c64, (1, 2)),
                            jnp.tile(s64, (1, 2)), cm, sm], axis=1)


def _pack_layer(w_in, w_pool, g_cq, w_uq, w_uk, w_br, w_out):
    d = w_in.shape[0]
    col = lambda name: w_in[:, _OFF[name][0]:_OFF[name][1]]
    z = lambda w: jnp.zeros((d, w), w_in.dtype)
    swa_q = col('swa_q')
    swa_blocks = []
    for j in range(SWA_HEADS):
        qj = swa_q[:, SWA_HEAD * j:SWA_HEAD * (j + 1)]
        swa_blocks += [qj, z(SWA_HEAD)] if j // 2 == 0 else [z(SWA_HEAD), qj]
    wa = jnp.concatenate(
        [col('pool_in'), col('mla_cq'), z(256 - MLA_Q_RANK), col('mla_ckv'),
         z(MLA_NOPE), col('mla_kr'), z(LANE - MLA_NOPE - MLA_ROPE),
         col('diff_q'), col('diff_k'), col('diff_v')] + swa_blocks + [col('swa_k'), col('swa_v')], axis=1)
    hq = MLA_NOPE + MLA_ROPE
    uq = jnp.pad(w_uq.reshape(MLA_Q_RANK, MLA_HEADS, hq), ((0, 256 - MLA_Q_RANK), (0, 0), (0, LANE - hq)))
    uk = jnp.pad(w_uk.reshape(MLA_KV_RANK, MLA_HEADS, MLA_NOPE), ((0, 0), (0, 0), (0, LANE - MLA_NOPE)))
    wpool = jnp.zeros((BRANCH_W, BRANCH_W), w_pool.dtype)
    for g in range(len(POOL_WINDOWS)):
        sl = slice(POOL_GROUP * g, POOL_GROUP * (g + 1))
        wpool = wpool.at[sl, sl].set(w_pool[g])
    return dict(
        wa=wa.astype(BF16), wg=col('gates').astype(BF16), wm=col('merge').astype(BF16),
        g_cq=jnp.pad(g_cq, (0, 256 - MLA_Q_RANK)).reshape(1, 256),
        w_uq=uq.reshape(256, MLA_HEADS * LANE).astype(BF16), w_uk=uk.reshape(MLA_KV_RANK, MLA_HEADS * LANE).astype(BF16),
        wpool=wpool.astype(BF16), w_br=w_br.astype(BF16), w_out=w_out.astype(BF16))


def kernel(x, c, ctx, c_ctx, w_mod, b_mod, g_pre, g_post, w_in, w_pool, s_pool, g_cq, w_uq, g_ckv, w_uk, w_uv,
           lam_q1, lam_k1, lam_q2, lam_k2, g_diff, sink, w_br, w_out):
    b, n, d = x.shape
    n_ctx = ctx.shape[1]
    depth = w_in.shape[0]
    tile = min(SEQ_TILE, n)
    assert n % tile == 0 and n_ctx % LANE == 0 and n_ctx <= SEQ_TILE and d % LANE == 0

    rows = -(-(b + 1) // 8) * 8
    cc = jnp.zeros((rows, d), F32).at[:b].set(c).at[b].set(c_ctx)
    mod_all = _mod_call(cc, w_mod, b_mod)
    tabs = _rope_tables(n, False)
    tabs_ctx = _rope_tables(n_ctx, True)

    xc = ctx
    for l in range(depth):
        need_ctx = l < depth - 1
        lam_init = 0.8 - 0.6 * math.exp(-0.3 * l)
        p = _pack_layer(w_in[l], w_pool[l], g_cq[l], w_uq[l], w_uk[l], w_br[l], w_out[l])
        mod = mod_all[l, :b].reshape(b, 1, 3 * d)
        mod_c = jnp.broadcast_to(mod_all[l, b].reshape(1, 1, 3 * d), (b, 1, 3 * d))
        gp, gpo = g_pre[l].reshape(1, d), g_post[l].reshape(1, d)
        gckv = g_ckv[l].reshape(1, MLA_KV_RANK)
        wuv = w_uv[l].astype(BF16)
        lam_v = jnp.stack([lam_q1[l], lam_k1[l], lam_q2[l], lam_k2[l]])
        gd = g_diff[l].reshape(DIFF_V, 1)
        sk = sink[l].reshape(1, SWA_HEADS)
        sp = s_pool[l].reshape(1, BRANCH_W)

        proj = lambda xx, mm, tt, tl: _proj_call(xx, mm, gp, p['wa'], tt, p['g_cq'], p['w_uq'], gckv, p['w_uk'],
                                                 wuv, tl)
        u, mq, mk, mvt, dq, dk, dvt, sq, skk, sv = proj(x, mod, tabs, tile)
        uc, mqc, mkc, mvtc, dqc, dkc, dvtc, sqc, skc, svc = proj(xc, mod_c, tabs_ctx, n_ctx)

        y_mla = _dense_call("mla", mq, mk, mvt, mkc, mvtc, (), lam_init)
        y_diff = _dense_call("diff", dq, dk, dvt, dkc, dvtc, (lam_v, gd), lam_init)
        y_swa = _swa_call(sq, skk, sv, skc, svc, sk, tile)
        merge = lambda xx, mm, uu, ya, yb, yc, tl: _merge_call(xx, mm, gp, gpo, p['wg'], p['wm'], uu, p['wpool'], sp,
                                                              ya, yb, yc, p['w_br'], p['w_out'], tl)
        x_new = merge(x, mod, u, y_mla, y_diff, y_swa, tile)
        if need_ctx:
            yc_mla = _dense_call("mla", mqc, None, None, mkc, mvtc, (), lam_init)
            yc_diff = _dense_call("diff", dqc, None, None, dkc, dvtc, (lam_v, gd), lam_init)
            yc_swa = _swa_call(sqc, None, None, skc, svc, sk, n_ctx)
            xc = merge(xc, mod_c, uc, yc_mla, yc_diff, yc_swa, n_ctx)
        x = x_new
    return x
```

```python
import functools
import math

import jax
import jax.numpy as jnp
from jax import lax
from jax.experimental import pallas as pl
from jax.experimental.pallas import tpu as pltpu

F32 = jnp.float32
BF16 = jnp.bfloat16

GRID_W = 64
EPS = 1e-6
ROPE_BASE = 10000.0
NEG = -1e30
N_BRANCH = 4
BRANCH_W = 256
POOL_WINDOWS = (2, 4, 8, 16)
POOL_GROUP = BRANCH_W // len(POOL_WINDOWS)
POOL_HALO = 8
MLA_HEADS, MLA_NOPE, MLA_ROPE, MLA_V = 4, 64, 32, 64
MLA_Q_RANK, MLA_KV_RANK = 192, 128
DIFF_HEADS, DIFF_QK, DIFF_V = 4, 32, 64
SWA_HEADS, SWA_KV_HEADS, SWA_HEAD, SWA_WINDOW = 4, 2, 64, 128
HEAD_V = 64
LOG2E = math.log2(math.e)

LANE = 128
ONES_ROWS = 16
VMEM_LIMIT = 56 * 1024 * 1024
SEQ_TILE = 512
KEY_CHUNK = 512
Q_TILE = {"mla": 512, "diff": 256}

_SPLITS = (('pool_in', 256), ('mla_cq', 192), ('mla_ckv', 128), ('mla_kr', 32), ('diff_q', 256), ('diff_k', 256),
           ('diff_v', 256), ('swa_q', 256), ('swa_k', 128), ('swa_v', 128), ('gates', 1024), ('merge', 4096))
_OFF = {}
_o = 0
for _n, _w in _SPLITS:
    _OFF[_n] = (_o, _o + _w)
    _o += _w

A_POOL, A_CQ, A_CKV, A_KR, A_DQ, A_DK, A_DV, A_SQ, A_SK, A_SV, A_END = (
    0, 256, 512, 640, 768, 1024, 1280, 1536, 2048, 2176, 2304)


def _cparams(sem):
    return pltpu.CompilerParams(dimension_semantics=sem, vmem_limit_bytes=VMEM_LIMIT)


def _const_spec(shape):
    nd = len(shape)
    return pl.BlockSpec(shape, lambda *_: (0,) * nd, pipeline_mode=pl.Buffered(1))


def _rms(x, g, n):
    return x * lax.rsqrt(jnp.sum(x * x, axis=-1, keepdims=True) * (1.0 / n) + EPS) * g


def _prenorm(x, mod, g_pre):
    d = x.shape[-1]
    shift, scale = mod[:, :d], mod[:, d:2 * d]
    return _rms(x, g_pre, d) * (1.0 + scale) + shift


def _rope128(x, c, s, period, half, first_below):
    lane = lax.broadcasted_iota(jnp.int32, x.shape, 1)
    first = (lane % period) < first_below
    swapped = jnp.where(first, pltpu.roll(x, LANE - half, 1), pltpu.roll(x, half, 1))
    return x * c + swapped * s


def _rope_mla(x, c, s):
    return _rope128(x, c, s, LANE, MLA_ROPE // 2, MLA_NOPE + MLA_ROPE // 2)


def _rope_heads(x, c, s, head_dim):
    return _rope128(x, c, s, head_dim, head_dim // 2, head_dim // 2)


def _mod_kernel(c_ref, w_ref, b_ref, o_ref):
    a = c_ref[...]
    a = a * jax.nn.sigmoid(a)
    w = w_ref[0]
    a_hi = a.astype(BF16)
    a_lo = (a - a_hi.astype(F32)).astype(BF16)
    w_hi = w.astype(BF16)
    w_lo = (w - w_hi.astype(F32)).astype(BF16)
    acc = jnp.dot(a_hi, w_hi, preferred_element_type=F32)
    acc += jnp.dot(a_lo, w_hi, preferred_element_type=F32)
    acc += jnp.dot(a_hi, w_lo, preferred_element_type=F32)
    o_ref[0] = acc + b_ref[0]


def _mod_call(cc, w_mod, b_mod):
    n_layer, d, d3 = w_mod.shape
    rows = cc.shape[0]
    return pl.pallas_call(
        _mod_kernel,
        grid=(n_layer, d3 // d),
        in_specs=[pl.BlockSpec((rows, d), lambda l, j: (0, 0)),
                  pl.BlockSpec((1, d, d), lambda l, j: (l, 0, j)),
                  pl.BlockSpec((1, 1, d), lambda l, j: (l, 0, j))],
        out_specs=pl.BlockSpec((1, rows, d), lambda l, j: (l, 0, j)),
        out_shape=jax.ShapeDtypeStruct((n_layer, rows, d3), F32),
        compiler_params=_cparams(("arbitrary", "arbitrary")),
        name="mod",
    )(cc, w_mod, b_mod.reshape(n_layer, 1, d3))


def _proj_kernel(x_ref, mod_ref, gpre_ref, wa_ref, tabs_ref, gcq_ref, wuq_ref, gckv_ref, wuk_ref, wuv_ref,
                 u_ref, mq_ref, mk_ref, mvt_ref, dq_ref, dk_ref, dvt_ref, sq_ref, sk_ref, sv_ref):
    hb = _prenorm(x_ref[0], mod_ref[0], gpre_ref[...]).astype(BF16)

    def sect(lo, hi):
        return jnp.dot(hb, wa_ref[:, lo:hi], preferred_element_type=F32)

    tabs = tabs_ref[...]
    c32, s32, c64, s64, cm, sm = (tabs[:, LANE * j:LANE * (j + 1)] for j in range(6))

    u_ref[0] = sect(A_POOL, A_CQ)

    cqn = _rms(sect(A_CQ, A_CKV), gcq_ref[...], MLA_Q_RANK).astype(BF16)
    q = jnp.dot(cqn, wuq_ref[...], preferred_element_type=F32)
    mla_scale = (MLA_NOPE + MLA_ROPE) ** -0.5 * LOG2E
    for h in range(MLA_HEADS):
        blk = _rope_mla(q[:, LANE * h:LANE * (h + 1)], cm, sm)
        mq_ref[0, :, LANE * h:LANE * (h + 1)] = (blk * mla_scale).astype(BF16)
    ckvn = _rms(sect(A_CKV, A_KR), gckv_ref[...], MLA_KV_RANK).astype(BF16)
    k_nope = jnp.dot(ckvn, wuk_ref[...], preferred_element_type=F32)
    k_rope = sect(A_KR, A_DQ)
    for h in range(MLA_HEADS):
        blk = _rope_mla(k_nope[:, LANE * h:LANE * (h + 1)] + k_rope, cm, sm)
        mk_ref[0, h] = blk.astype(BF16)
    mvt_ref[0, 0] = jnp.dot(ckvn, wuv_ref[...], preferred_element_type=F32).T.astype(BF16)

    dq = sect(A_DQ, A_DK)
    dk = sect(A_DK, A_DV)
    diff_scale = DIFF_QK ** -0.5 * LOG2E
    for j in range(2):
        sl = slice(LANE * j, LANE * (j + 1))
        dq_ref[0, :, sl] = (_rope_heads(dq[:, sl], c32, s32, DIFF_QK) * diff_scale).astype(BF16)
        dk_ref[0, :, sl] = _rope_heads(dk[:, sl], c32, s32, DIFF_QK).astype(BF16)
    dvt_ref[0, 0] = sect(A_DV, A_SQ).T.astype(BF16)

    sq = sect(A_SQ, A_SK)
    swa_scale = SWA_HEAD ** -0.5
    for j in range(SWA_HEADS):
        sl = slice(LANE * j, LANE * (j + 1))
        sq_ref[0, :, sl] = (_rope_heads(sq[:, sl], c64, s64, SWA_HEAD) * swa_scale).astype(BF16)
    sk_ref[0] = _rope_heads(sect(A_SK, A_SV), c64, s64, SWA_HEAD).astype(BF16)
    sv_ref[0] = sect(A_SV, A_END).astype(BF16)


def _proj_call(x, mod, g_pre, wa, tabs, g_cq, w_uq, g_ckv, w_uk, w_uv, tile):
    b, n, d = x.shape
    nt = n // tile
    row = lambda w: pl.BlockSpec((1, tile, w), lambda i, bb: (bb, i, 0))
    vt = pl.BlockSpec((1, 1, 2 * LANE, tile), lambda i, bb: (bb, i, 0, 0))
    sds = jax.ShapeDtypeStruct
    return pl.pallas_call(
        _proj_kernel,
        grid=(nt, b),
        in_specs=[row(d),
                  pl.BlockSpec((1, 1, 3 * d), lambda i, bb: (bb, 0, 0)),
                  _const_spec(g_pre.shape), _const_spec(wa.shape),
                  pl.BlockSpec((tile, 6 * LANE), lambda i, bb: (i, 0)),
                  _const_spec(g_cq.shape), _const_spec(w_uq.shape), _const_spec(g_ckv.shape),
                  _const_spec(w_uk.shape), _const_spec(w_uv.shape)],
        out_specs=[row(256), row(512), pl.BlockSpec((1, MLA_HEADS, tile, LANE), lambda i, bb: (bb, 0, i, 0)), vt,
                   row(256), row(256), vt, row(512), row(128), row(128)],
        out_shape=[sds((b, n, 256), F32), sds((b, n, 512), BF16), sds((b, MLA_HEADS, n, LANE), BF16),
                   sds((b, nt, 256, tile), BF16), sds((b, n, 256), BF16), sds((b, n, 256), BF16),
                   sds((b, nt, 256, tile), BF16), sds((b, n, 512), BF16), sds((b, n, 128), BF16),
                   sds((b, n, 128), BF16)],
        compiler_params=_cparams(("arbitrary", "arbitrary")),
        name="proj",
    )(x, mod, g_pre, wa, tabs, g_cq, w_uq, g_ckv, w_uk, w_uv)


def _dense_kernel(*refs, kind, has_lat, lam_init, n_chunks, tk):
    refs = list(refs)
    q_ref = refs.pop(0)
    if has_lat:
        klat_ref, vlat_ref = refs.pop(0), refs.pop(0)
    kctx_ref, vctx_ref = refs.pop(0), refs.pop(0)
    if kind == "diff":
        lam_ref, gd_ref = refs.pop(0), refs.pop(0)
    o_ref = refs.pop(0)
    tq = q_ref.shape[1]
    if has_lat:
        vt_len = vlat_ref.shape[3]
        sub = tk // vt_len

    chains = []
    for j in range(2):
        if kind == "mla":
            chains.append((j, q_ref[0, :, LANE * j:LANE * (j + 1)]))
        else:
            q_all = q_ref[0]
            lane = lax.broadcasted_iota(jnp.int32, q_all.shape, 1)
            for m in range(2):
                lo = 2 * DIFF_QK * j + DIFF_QK * m
                chains.append((j, jnp.where((lane >= lo) & (lane < lo + DIFF_QK), q_all, jnp.zeros_like(q_all))))

    def keys(ref, j, rws):
        return ref[0, j, rws, :] if kind == "mla" else ref[0, rws, :]

    def score(kcs):
        ss = [lax.dot_general(kcs[j], qm, (((1,), (1,)), ((), ())), preferred_element_type=F32) for j, qm in chains]
        return ss, [jnp.max(s, axis=0, keepdims=True) for s in ss]

    def absorb(state, ss, maxes, vtcs):
        new = []
        for (m_run, acc), (j, _), s, mx in zip(state, chains, ss, maxes):
            m_new = jnp.maximum(m_run, mx)
            p = jnp.exp2(s - m_new).astype(BF16)
            new.append((m_new, jnp.exp2(m_run - m_new) * acc + jnp.dot(vtcs[j], p, preferred_element_type=F32)))
        return tuple(new)

    def ext(vt):
        return jnp.concatenate([vt, jnp.ones((ONES_ROWS, vt.shape[1]), BF16)], axis=0)

    def lat_keys(c):
        start = pl.multiple_of(c * tk, tk)
        return [keys(klat_ref, j, pl.ds(start, tk)) for j in range(2)]

    def lat_vals(c):
        return [[ext(vlat_ref[0, c * sub + u, HEAD_V * j:HEAD_V * (j + 1), :]) for u in range(sub)] for j in range(2)]

    state = tuple((jnp.full((1, tq), NEG, F32), jnp.zeros((HEAD_V + ONES_ROWS, tq), F32)) for _ in chains)
    ss_ctx, mx_ctx = score([keys(kctx_ref, j, slice(None)) for j in range(2)])
    vts_ctx = [ext(vctx_ref[0, 0, HEAD_V * j:HEAD_V * (j + 1), :]) for j in range(2)]
    if not has_lat:
        state = absorb(state, ss_ctx, mx_ctx, vts_ctx)
    else:
        nch = len(chains)
        s_a, s_b, p_a, p_b = (refs[nch * g:nch * (g + 1)] for g in range(4))

        def park(s_buf, c):
            ss, maxes = score(lat_keys(c))
            for ch, s in enumerate(ss):
                s_buf[ch][...] = s
            return tuple(maxes)

        def soften(s_buf, p_buf, ms, maxes):
            new_ms, alphas = [], []
            for ch, (m_run, mx) in enumerate(zip(ms, maxes)):
                m_new = jnp.maximum(m_run, mx)
                p_buf[ch][...] = jnp.exp2(s_buf[ch][...] - m_new).astype(BF16)
                new_ms.append(m_new)
                alphas.append(jnp.exp2(m_run - m_new))
            return tuple(new_ms), tuple(alphas)

        def weigh(accs, alphas, p_buf, c):
            vts = lat_vals(c)
            out = []
            for ch, (acc, a, (j, _)) in enumerate(zip(accs, alphas, chains)):
                acc = a * acc
                for u in range(sub):
                    acc = acc + jnp.dot(vts[j][u], p_buf[ch][vt_len * u:vt_len * (u + 1), :],
                                        preferred_element_type=F32)
                out.append(acc)
            return tuple(out)

        mx0 = park(s_a, 0)
        mx_cur = park(s_b, 1)
        state = absorb(state, ss_ctx, mx_ctx, vts_ctx)
        ms, accs = tuple(m for m, _ in state), tuple(a for _, a in state)
        ms, al_prev = soften(s_a, p_a, ms, mx0)

        def body(i, carry):
            ms, accs, al_prev, mx_cur = carry
            c = 2 * i + 1
            ms, al_cur = soften(s_b, p_b, ms, mx_cur)
            mx_nxt = park(s_a, c + 1)
            accs = weigh(accs, al_prev, p_a, c - 1)
            ms, al_prev = soften(s_a, p_a, ms, mx_nxt)
            mx_cur = park(s_b, c + 2)
            accs = weigh(accs, al_cur, p_b, c)
            return ms, accs, al_prev, mx_cur

        carry = (ms, accs, al_prev, mx_cur)
        for i in range(n_chunks // 2 - 1):
            carry = body(i, carry)
        ms, accs, al_prev, mx_cur = carry
        ms, al_cur = soften(s_b, p_b, ms, mx_cur)
        accs = weigh(accs, al_prev, p_a, n_chunks - 2)
        accs = weigh(accs, al_cur, p_b, n_chunks - 1)
        state = tuple(zip(ms, accs))

    normed = [acc[:HEAD_V] / acc[HEAD_V:HEAD_V + 1] for _, acc in state]
    if kind == "mla":
        outs = normed
    else:
        lam_v = lam_ref[...]
        lam = (jnp.exp(jnp.sum(lam_v[0:1] * lam_v[1:2], axis=1, keepdims=True))
               - jnp.exp(jnp.sum(lam_v[2:3] * lam_v[3:4], axis=1, keepdims=True)) + lam_init)
        outs = []
        for j in range(2):
            o = normed[2 * j] - lam * normed[2 * j + 1]
            o = o * lax.rsqrt(jnp.sum(o * o, axis=0, keepdims=True) * (1.0 / DIFF_V) + EPS)
            outs.append(o * gd_ref[...] * (1.0 - lam_init))
    o_ref[0] = jnp.concatenate(outs, axis=0).T.astype(BF16)


def _dense_call(kind, q, k_lat, vt_lat, k_ctx, vt_ctx, extra, lam_init):
    b, n, _ = q.shape
    tq = min(Q_TILE[kind], n)
    assert n % tq == 0
    w = 2 * LANE if kind == "mla" else LANE
    has_lat = k_lat is not None
    c = vt_ctx.shape[3]

    def key_spec(rows):
        if kind == "mla":
            return pl.BlockSpec((1, 2, rows, LANE), lambda bb, hp, i: (bb, hp, 0, 0))
        return pl.BlockSpec((1, rows, LANE), lambda bb, hp, i: (bb, 0, hp))

    args, specs = [q], [pl.BlockSpec((1, tq, w), lambda bb, hp, i: (bb, i, hp))]
    n_chunks = tk = 0
    scratch = []
    if has_lat:
        n_lat, vt_len = vt_lat.shape[1] * vt_lat.shape[3], vt_lat.shape[3]
        tk = KEY_CHUNK if n_lat % (2 * KEY_CHUNK) == 0 else vt_len
        n_chunks = n_lat // tk
        assert tk % vt_len == 0 and n_chunks % 2 == 0
        n_chains = 2 if kind == "mla" else 4
        scratch = [pltpu.VMEM((tk, tq), F32)] * (2 * n_chains) + [pltpu.VMEM((tk, tq), BF16)] * (2 * n_chains)
        args += [k_lat, vt_lat]
        specs += [key_spec(n_lat), pl.BlockSpec((1, n_lat // vt_len, LANE, vt_len), lambda bb, hp, i: (bb, 0, hp, 0))]
    args += [k_ctx, vt_ctx]
    specs += [key_spec(c), pl.BlockSpec((1, 1, LANE, c), lambda bb, hp, i: (bb, 0, hp, 0))]
    for e in extra:
        args.append(e)
        specs.append(pl.BlockSpec(e.shape, lambda bb, hp, i: (0, 0)))
    return pl.pallas_call(
        functools.partial(_dense_kernel, kind=kind, has_lat=has_lat, lam_init=lam_init, n_chunks=n_chunks, tk=tk),
        grid=(b, 2, n // tq),
        in_specs=specs,
        out_specs=pl.BlockSpec((1, tq, LANE), lambda bb, hp, i: (bb, i, hp)),
        out_shape=jax.ShapeDtypeStruct((b, n, 2 * LANE), BF16),
        scratch_shapes=scratch,
        compiler_params=_cparams(("arbitrary", "arbitrary", "arbitrary")),
        name="dense_" + kind + ("_lat" if has_lat else "_ctx"),
    )(*args)


def _swa_kernel(*refs, has_win, n_total):
    refs = list(refs)
    q_ref = refs.pop(0)
    if has_win:
        kp_ref, km_ref, kn_ref, vp_ref, vm_ref, vn_ref = (refs.pop(0) for _ in range(6))
    kc_ref, vc_ref, sink_ref, o_ref = refs
    tq = q_ref.shape[1]
    nt = (((1,), (1,)), ((), ()))
    kc, vc = kc_ref[0], vc_ref[0]
    if has_win:
        t0 = pl.program_id(1) * tq
        kw = jnp.concatenate([kp_ref[0], km_ref[0], kn_ref[0]], axis=0)
        vw = jnp.concatenate([vp_ref[0], vm_ref[0], vn_ref[0]], axis=0)
        nk = kw.shape[0]
        qpos = t0 + lax.broadcasted_iota(jnp.int32, (tq, nk), 0)
        kpos = t0 - SWA_WINDOW + lax.broadcasted_iota(jnp.int32, (tq, nk), 1)
        dist = qpos - kpos
        visible = (dist <= SWA_WINDOW) & (dist >= -SWA_WINDOW) & (kpos >= 0) & (kpos < n_total)
    sink = sink_ref[...]
    res = []
    for j in range(SWA_HEADS):
        qj = q_ref[0, :, LANE * j:LANE * (j + 1)]
        sink_j = sink[:, j:j + 1]
        s_c = lax.dot_general(qj, kc, nt, preferred_element_type=F32)
        m = jnp.maximum(jnp.max(s_c, axis=1, keepdims=True), sink_j)
        if has_win:
            s_w = jnp.where(visible, lax.dot_general(qj, kw, nt, preferred_element_type=F32), NEG)
            m = jnp.maximum(m, jnp.max(s_w, axis=1, keepdims=True))
        p_c = jnp.exp(s_c - m)
        den = jnp.sum(p_c, axis=1, keepdims=True) + jnp.exp(sink_j - m)
        num = jnp.dot(p_c.astype(BF16), vc, preferred_element_type=F32)
        if has_win:
            p_w = jnp.exp(s_w - m)
            den = den + jnp.sum(p_w, axis=1, keepdims=True)
            num = num + jnp.dot(p_w.astype(BF16), vw, preferred_element_type=F32)
        res.append(num / den)
    lane = lax.broadcasted_iota(jnp.int32, (tq, LANE), 1)
    low = lane < SWA_HEAD
    o_ref[0, :, 0:LANE] = jnp.where(low, res[0], pltpu.roll(res[1], SWA_HEAD, 1)).astype(BF16)
    o_ref[0, :, LANE:2 * LANE] = jnp.where(low, pltpu.roll(res[2], SWA_HEAD, 1), res[3]).astype(BF16)


def _swa_call(q, k, v, k_ctx, v_ctx, sink, tile):
    b, n, _ = q.shape
    has_win = k is not None
    c = k_ctx.shape[1]
    args, specs = [q], [pl.BlockSpec((1, tile, 4 * LANE), lambda bb, i: (bb, i, 0))]
    if has_win:
        r = tile // SWA_WINDOW
        last = n // SWA_WINDOW - 1
        prev = pl.BlockSpec((1, SWA_WINDOW, LANE), lambda bb, i: (bb, jnp.maximum(i * r - 1, 0), 0))
        main = pl.BlockSpec((1, tile, LANE), lambda bb, i: (bb, i, 0))
        nxt = pl.BlockSpec((1, SWA_WINDOW, LANE), lambda bb, i: (bb, jnp.minimum((i + 1) * r, last), 0))
        args += [k, k, k, v, v, v]
        specs += [prev, main, nxt, prev, main, nxt]
    args += [k_ctx, v_ctx, sink]
    specs += [pl.BlockSpec((1, c, LANE), lambda bb, i: (bb, 0, 0)),
              pl.BlockSpec((1, c, LANE), lambda bb, i: (bb, 0, 0)),
              pl.BlockSpec(sink.shape, lambda bb, i: (0, 0))]
    return pl.pallas_call(
        functools.partial(_swa_kernel, has_win=has_win, n_total=n),
        grid=(b, n // tile),
        in_specs=specs,
        out_specs=pl.BlockSpec((1, tile, 2 * LANE), lambda bb, i: (bb, i, 0)),
        out_shape=jax.ShapeDtypeStruct((b, n, 2 * LANE), BF16),
        compiler_params=_cparams(("arbitrary", "arbitrary")),
        name="swa_win" if has_win else "swa_ctx",
    )(*args)


def _pool(u, u_prev, u_next, t0, n_total):
    tile = u.shape[0]
    ext = jnp.concatenate([jnp.where(t0 > 0, u_prev, 0.0), u, jnp.where(t0 + tile < n_total, u_next, 0.0)], axis=0)
    n_ext = ext.shape[0]

    def ahead(a, k):
        return pltpu.roll(a, n_ext - k, 0) if k else a

    sums, run, w = [], ext, 1
    for win in POOL_WINDOWS:
        while w < win:
            run = run + ahead(run, w)
            w *= 2
        sums.append(ahead(run, POOL_HALO - win // 2)[:tile])
    grp = lax.broadcasted_iota(jnp.int32, (tile, BRANCH_W), 1) // POOL_GROUP
    t = t0 + lax.broadcasted_iota(jnp.int32, (tile, BRANCH_W), 0)
    win_sum, half = sums[-1], jnp.full((tile, BRANCH_W), POOL_WINDOWS[-1] // 2, jnp.int32)
    for g in range(len(POOL_WINDOWS) - 2, -1, -1):
        win_sum = jnp.where(grp == g, sums[g], win_sum)
        half = jnp.where(grp == g, POOL_WINDOWS[g] // 2, half)
    lo = jnp.clip(t - half, 0, n_total)
    hi = jnp.clip(t + half, 0, n_total)
    return win_sum / (hi - lo).astype(F32) - u


def _merge_kernel(x_ref, mod_ref, gpre_ref, gpost_ref, wg_ref, wm_ref, up_ref, u_ref, un_ref, wpool_ref, spool_ref,
                  ymla_ref, ydiff_ref, yswa_ref, wbr_ref, wout_ref, o_ref, *, n_total):
    x = x_ref[0]
    d = x.shape[-1]
    tile = x.shape[0]
    mod = mod_ref[0]
    hb = _prenorm(x, mod, gpre_ref[...]).astype(BF16)

    pooled = _pool(u_ref[0], up_ref[0], un_ref[0], pl.program_id(0) * tile, n_total)
    y_pool = jnp.dot(pooled.astype(BF16), wpool_ref[...], preferred_element_type=F32) * spool_ref[...]
    ys = [y_pool, ymla_ref[0].astype(F32), ydiff_ref[0].astype(F32), yswa_ref[0].astype(F32)]

    merged = jnp.zeros((tile, d), F32)
    for r in range(N_BRANCH):
        g = jnp.dot(hb, wg_ref[:, BRANCH_W * r:BRANCH_W * (r + 1)], preferred_element_type=F32)
        g = g * jax.nn.sigmoid(g)
        mg = jax.nn.sigmoid(jnp.dot(hb, wm_ref[:, d * r:d * (r + 1)], preferred_element_type=F32))
        merged = merged + mg * jnp.dot((ys[r] * g).astype(BF16), wbr_ref[r], preferred_element_type=F32)
    out = jnp.dot(merged.astype(BF16), wout_ref[...], preferred_element_type=F32)
    o_ref[0] = x + mod[:, 2 * d:] * _rms(out, gpost_ref[...], d)


def _merge_call(x, mod, g_pre, g_post, wg, wm, u, wpool, s_pool, y_mla, y_diff, y_swa, w_br, w_out, tile):
    b, n, d = x.shape
    r = tile // POOL_HALO
    last = n // POOL_HALO - 1
    row = lambda w: pl.BlockSpec((1, tile, w), lambda i, bb: (bb, i, 0))
    return pl.pallas_call(
        functools.partial(_merge_kernel, n_total=n),
        grid=(n // tile, b),
        in_specs=[row(d),
                  pl.BlockSpec((1, 1, 3 * d), lambda i, bb: (bb, 0, 0)),
                  _const_spec(g_pre.shape), _const_spec(g_post.shape), _const_spec(wg.shape), _const_spec(wm.shape),
                  pl.BlockSpec((1, POOL_HALO, BRANCH_W), lambda i, bb: (bb, jnp.maximum(i * r - 1, 0), 0)),
                  row(BRANCH_W),
                  pl.BlockSpec((1, POOL_HALO, BRANCH_W), lambda i, bb: (bb, jnp.minimum((i + 1) * r, last), 0)),
                  _const_spec(wpool.shape), _const_spec(s_pool.shape),
                  row(BRANCH_W), row(BRANCH_W), row(BRANCH_W),
                  _const_spec(w_br.shape), _const_spec(w_out.shape)],
        out_specs=row(d),
        out_shape=jax.ShapeDtypeStruct((b, n, d), F32),
        compiler_params=_cparams(("arbitrary", "arbitrary")),
        name="merge",
    )(x, mod, g_pre, g_post, wg, wm, u, u, u, wpool, s_pool, y_mla, y_diff, y_swa, w_br, w_out)


def _rope_tables(n_tok, identity):
    if identity:
        one, zero = jnp.ones((n_tok, LANE), F32), jnp.zeros((n_tok, LANE), F32)
        return jnp.concatenate([one, zero, one, zero, one, zero], axis=1)
    t = jnp.arange(n_tok, dtype=jnp.int32)
    row = (t // GRID_W).astype(F32)[:, None]
    col = (t % GRID_W).astype(F32)[:, None]

    def cs(rot_dim):
        n_freq = rot_dim // 4
        inv = jnp.exp(-math.log(ROPE_BASE) * jnp.arange(n_freq, dtype=F32) / n_freq)
        ang = jnp.concatenate([row * inv, col * inv], axis=-1)
        c, s = jnp.cos(ang), jnp.sin(ang)
        return jnp.concatenate([c, c], axis=1), jnp.concatenate([-s, s], axis=1)

    c32, s32 = cs(DIFF_QK)
    c64, s64 = cs(SWA_HEAD)
    one, zero = jnp.ones((n_tok, MLA_NOPE), F32), jnp.zeros((n_tok, MLA_NOPE), F32)
    cm = jnp.concatenate([one, c32, one[:, :MLA_ROPE]], axis=1)
    sm = jnp.concatenate([zero, s32, zero[:, :MLA_ROPE]], axis=1)
    return jnp.concatenate([jnp.tile(c32, (1, 4)), jnp.tile(s32, (1, 4)), jnp.tile(c64, (1, 2)),
                            jnp.tile(s64, (1, 2)), cm, sm], axis=1)


def _pack_layer(w_in, w_pool, g_cq, w_uq, w_uk, w_br, w_out):
    d = w_in.shape[0]
    col = lambda name: w_in[:, _OFF[name][0]:_OFF[name][1]]
    z = lambda w: jnp.zeros((d, w), w_in.dtype)
    swa_q = col('swa_q')
    swa_blocks = []
    for j in range(SWA_HEADS):
        qj = swa_q[:, SWA_HEAD * j:SWA_HEAD * (j + 1)]
        swa_blocks += [qj, z(SWA_HEAD)] if j // 2 == 0 else [z(SWA_HEAD), qj]
    wa = jnp.concatenate(
        [col('pool_in'), col('mla_cq'), z(256 - MLA_Q_RANK), col('mla_ckv'),
         z(MLA_NOPE), col('mla_kr'), z(LANE - MLA_NOPE - MLA_ROPE),
         col('diff_q'), col('diff_k'), col('diff_v')] + swa_blocks + [col('swa_k'), col('swa_v')], axis=1)
    hq = MLA_NOPE + MLA_ROPE
    uq = jnp.pad(w_uq.reshape(MLA_Q_RANK, MLA_HEADS, hq), ((0, 256 - MLA_Q_RANK), (0, 0), (0, LANE - hq)))
    uk = jnp.pad(w_uk.reshape(MLA_KV_RANK, MLA_HEADS, MLA_NOPE), ((0, 0), (0, 0), (0, LANE - MLA_NOPE)))
    wpool = jnp.zeros((BRANCH_W, BRANCH_W), w_pool.dtype)
    for g in range(len(POOL_WINDOWS)):
        sl = slice(POOL_GROUP * g, POOL_GROUP * (g + 1))
        wpool = wpool.at[sl, sl].set(w_pool[g])
    return dict(
        wa=wa.astype(BF16), wg=col('gates').astype(BF16), wm=col('merge').astype(BF16),
        g_cq=jnp.pad(g_cq, (0, 256 - MLA_Q_RANK)).reshape(1, 256),
        w_uq=uq.reshape(256, MLA_HEADS * LANE).astype(BF16), w_uk=uk.reshape(MLA_KV_RANK, MLA_HEADS * LANE).astype(BF16),
        wpool=wpool.astype(BF16), w_br=w_br.astype(BF16), w_out=w_out.astype(BF16))


def kernel(x, c, ctx, c_ctx, w_mod, b_mod, g_pre, g_post, w_in, w_pool, s_pool, g_cq, w_uq, g_ckv, w_uk, w_uv,
           lam_q1, lam_k1, lam_q2, lam_k2, g_diff, sink, w_br, w_out):
    b, n, d = x.shape
    n_ctx = ctx.shape[1]
    depth = w_in.shape[0]
    tile = min(SEQ_TILE, n)
    assert n % tile == 0 and n_ctx % LANE == 0 and n_ctx <= SEQ_TILE and d % LANE == 0

    rows = -(-(b + 1) // 8) * 8
    cc = jnp.zeros((rows, d), F32).at[:b].set(c).at[b].set(c_ctx)
    mod_all = _mod_call(cc, w_mod, b_mod)
    tabs = _rope_tables(n, False)
    tabs_ctx = _rope_tables(n_ctx, True)

    xc = ctx
    for l in range(depth):
        need_ctx = l < depth - 1
        lam_init = 0.8 - 0.6 * math.exp(-0.3 * l)
        p = _pack_layer(w_in[l], w_pool[l], g_cq[l], w_uq[l], w_uk[l], w_br[l], w_out[l])
        mod = mod_all[l, :b].reshape(b, 1, 3 * d)
        mod_c = jnp.broadcast_to(mod_all[l, b].reshape(1, 1, 3 * d), (b, 1, 3 * d))
        gp, gpo = g_pre[l].reshape(1, d), g_post[l].reshape(1, d)
        gckv = g_ckv[l].reshape(1, MLA_KV_RANK)
        wuv = w_uv[l].astype(BF16)
        lam_v = jnp.stack([lam_q1[l], lam_k1[l], lam_q2[l], lam_k2[l]])
        gd = g_diff[l].reshape(DIFF_V, 1)
        sk = sink[l].reshape(1, SWA_HEADS)
        sp = s_pool[l].reshape(1, BRANCH_W)

        proj = lambda xx, mm, tt, tl: _proj_call(xx, mm, gp, p['wa'], tt, p['g_cq'], p['w_uq'], gckv, p['w_uk'],
                                                 wuv, tl)
        u, mq, mk, mvt, dq, dk, dvt, sq, skk, sv = proj(x, mod, tabs, tile)
        uc, mqc, mkc, mvtc, dqc, dkc, dvtc, sqc, skc, svc = proj(xc, mod_c, tabs_ctx, n_ctx)

        y_mla = _dense_call("mla", mq, mk, mvt, mkc, mvtc, (), lam_init)
        y_diff = _dense_call("diff", dq, dk, dvt, dkc, dvtc, (lam_v, gd), lam_init)
        y_swa = _swa_call(sq, skk, sv, skc, svc, sk, tile)
        merge = lambda xx, mm, uu, ya, yb, yc, tl: _merge_call(xx, mm, gp, gpo, p['wg'], p['wm'], uu, p['wpool'], sp,
                                                              ya, yb, yc, p['w_br'], p['w_out'], tl)
        x_new = merge(x, mod, u, y_mla, y_diff, y_swa, tile)
        if need_ctx:
            yc_mla = _dense_call("mla", mqc, None, None, mkc, mvtc, (), lam_init)
            yc_diff = _dense_call("diff", dqc, None, None, dkc, dvtc, (lam_v, gd), lam_init)
            yc_swa = _swa_call(sqc, None, None, skc, svc, sk, n_ctx)
            xc = merge(xc, mod_c, uc, yc_mla, yc_diff, yc_swa, n_ctx)
        x = x_new
    return x
```

```python
import functools
import math

import jax
import jax.numpy as jnp
from jax import lax
from jax.experimental import pallas as pl
from jax.experimental.pallas import tpu as pltpu

F32 = jnp.float32
BF16 = jnp.bfloat16

GRID_W = 64
EPS = 1e-6
ROPE_BASE = 10000.0
NEG = -1e30
N_BRANCH = 4
BRANCH_W = 256
POOL_WINDOWS = (2, 4, 8, 16)
POOL_GROUP = BRANCH_W // len(POOL_WINDOWS)
POOL_HALO = 8
MLA_HEADS, MLA_NOPE, MLA_ROPE, MLA_V = 4, 64, 32, 64
MLA_Q_RANK, MLA_KV_RANK = 192, 128
DIFF_HEADS, DIFF_QK, DIFF_V = 4, 32, 64
SWA_HEADS, SWA_KV_HEADS, SWA_HEAD, SWA_WINDOW = 4, 2, 64, 128
HEAD_V = 64
LOG2E = math.log2(math.e)

LANE = 128
ONES_ROWS = 16
VMEM_LIMIT = 56 * 1024 * 1024
SEQ_TILE = 512
SWA_TILE = 512
LOOKAHEAD = 1
KEY_CHUNK = {"mla": 256, "diff": 512}
Q_TILE = {"mla": 512, "diff": 256}

_SPLITS = (('pool_in', 256), ('mla_cq', 192), ('mla_ckv', 128), ('mla_kr', 32), ('diff_q', 256), ('diff_k', 256),
           ('diff_v', 256), ('swa_q', 256), ('swa_k', 128), ('swa_v', 128), ('gates', 1024), ('merge', 4096))
_OFF = {}
_o = 0
for _n, _w in _SPLITS:
    _OFF[_n] = (_o, _o + _w)
    _o += _w

A_POOL, A_CQ, A_CKV, A_KR, A_DQ, A_DK, A_DV, A_SQ, A_SK, A_SV, A_END = (
    0, 256, 512, 640, 768, 1024, 1280, 1536, 2048, 2176, 2304)


def _cparams(sem):
    return pltpu.CompilerParams(dimension_semantics=sem, vmem_limit_bytes=VMEM_LIMIT)


def _const_spec(shape):
    nd = len(shape)
    return pl.BlockSpec(shape, lambda *_: (0,) * nd, pipeline_mode=pl.Buffered(1))


def _sigmoid(x):
    return 0.5 * jnp.tanh(0.5 * x) + 0.5


def _rms(x, g, n):
    return x * lax.rsqrt(jnp.sum(x * x, axis=-1, keepdims=True) * (1.0 / n) + EPS) * g


def _prenorm(x, mod, g_pre):
    d = x.shape[-1]
    shift, scale = mod[:, :d], mod[:, d:2 * d]
    return _rms(x, g_pre, d) * (1.0 + scale) + shift


def _rope128(x, c, s, period, half, first_below):
    lane = lax.broadcasted_iota(jnp.int32, x.shape, 1)
    first = (lane % period) < first_below
    swapped = jnp.where(first, pltpu.roll(x, LANE - half, 1), pltpu.roll(x, half, 1))
    return x * c + swapped * s


def _rope_mla(x, c, s):
    return _rope128(x, c, s, LANE, MLA_ROPE // 2, MLA_NOPE + MLA_ROPE // 2)


def _rope_heads(x, c, s, head_dim):
    return _rope128(x, c, s, head_dim, head_dim // 2, head_dim // 2)


def _mod_kernel(c_ref, w_ref, b_ref, o_ref):
    a = c_ref[...]
    a = a * jax.nn.sigmoid(a)
    w = w_ref[0]
    a_hi = a.astype(BF16)
    a_lo = (a - a_hi.astype(F32)).astype(BF16)
    w_hi = w.astype(BF16)
    w_lo = (w - w_hi.astype(F32)).astype(BF16)
    acc = jnp.dot(a_hi, w_hi, preferred_element_type=F32)
    acc += jnp.dot(a_lo, w_hi, preferred_element_type=F32)
    acc += jnp.dot(a_hi, w_lo, preferred_element_type=F32)
    o_ref[0] = acc + b_ref[0]


def _mod_call(cc, w_mod, b_mod):
    n_layer, d, d3 = w_mod.shape
    rows = cc.shape[0]
    return pl.pallas_call(
        _mod_kernel,
        grid=(n_layer, d3 // d),
        in_specs=[pl.BlockSpec((rows, d), lambda l, j: (0, 0)),
                  pl.BlockSpec((1, d, d), lambda l, j: (l, 0, j)),
                  pl.BlockSpec((1, 1, d), lambda l, j: (l, 0, j))],
        out_specs=pl.BlockSpec((1, rows, d), lambda l, j: (l, 0, j)),
        out_shape=jax.ShapeDtypeStruct((n_layer, rows, d3), F32),
        compiler_params=_cparams(("arbitrary", "arbitrary")),
        name="mod",
    )(cc, w_mod, b_mod.reshape(n_layer, 1, d3))


def _proj_kernel(x_ref, mod_ref, gpre_ref, wa_ref, tabs_ref, gcq_ref, wuq_ref, gckv_ref, wuk_ref, wuv_ref,
                 u_ref, mq_ref, mk_ref, mvt_ref, dq_ref, dk_ref, dvt_ref, sq_ref, sk_ref, sv_ref):
    hb = _prenorm(x_ref[0], mod_ref[0], gpre_ref[...]).astype(BF16)

    def sect(lo, hi):
        return jnp.dot(hb, wa_ref[:, lo:hi], preferred_element_type=F32)

    tabs = tabs_ref[...]
    c32, s32, c64, s64, cm, sm = (tabs[:, LANE * j:LANE * (j + 1)] for j in range(6))

    u_ref[0] = sect(A_POOL, A_CQ)

    cqn = _rms(sect(A_CQ, A_CKV), gcq_ref[...], MLA_Q_RANK).astype(BF16)
    q = jnp.dot(cqn, wuq_ref[...], preferred_element_type=F32)
    mla_scale = (MLA_NOPE + MLA_ROPE) ** -0.5 * LOG2E
    for h in range(MLA_HEADS):
        blk = _rope_mla(q[:, LANE * h:LANE * (h + 1)], cm, sm)
        mq_ref[0, :, LANE * h:LANE * (h + 1)] = (blk * mla_scale).astype(BF16)
    ckvn = _rms(sect(A_CKV, A_KR), gckv_ref[...], MLA_KV_RANK).astype(BF16)
    k_nope = jnp.dot(ckvn, wuk_ref[...], preferred_element_type=F32)
    k_rope = sect(A_KR, A_DQ)
    for h in range(MLA_HEADS):
        blk = _rope_mla(k_nope[:, LANE * h:LANE * (h + 1)] + k_rope, cm, sm)
        mk_ref[0, h] = blk.astype(BF16)
    mvt_ref[0, 0] = jnp.dot(ckvn, wuv_ref[...], preferred_element_type=F32).T.astype(BF16)

    dq = sect(A_DQ, A_DK)
    dk = sect(A_DK, A_DV)
    diff_scale = DIFF_QK ** -0.5 * LOG2E
    for j in range(2):
        sl = slice(LANE * j, LANE * (j + 1))
        dq_ref[0, :, sl] = (_rope_heads(dq[:, sl], c32, s32, DIFF_QK) * diff_scale).astype(BF16)
        dk_ref[0, :, sl] = _rope_heads(dk[:, sl], c32, s32, DIFF_QK).astype(BF16)
    dvt_ref[0, 0] = sect(A_DV, A_SQ).T.astype(BF16)

    sq = sect(A_SQ, A_SK)
    swa_scale = SWA_HEAD ** -0.5
    for j in range(SWA_HEADS):
        sl = slice(LANE * j, LANE * (j + 1))
        sq_ref[0, :, sl] = (_rope_heads(sq[:, sl], c64, s64, SWA_HEAD) * swa_scale).astype(BF16)
    sk_ref[0] = _rope_heads(sect(A_SK, A_SV), c64, s64, SWA_HEAD).astype(BF16)
    sv_ref[0] = sect(A_SV, A_END).astype(BF16)


def _proj_call(x, mod, g_pre, wa, tabs, g_cq, w_uq, g_ckv, w_uk, w_uv, tile):
    b, n, d = x.shape
    nt = n // tile
    row = lambda w: pl.BlockSpec((1, tile, w), lambda i, bb: (bb, i, 0))
    vt = pl.BlockSpec((1, 1, 2 * LANE, tile), lambda i, bb: (bb, i, 0, 0))
    sds = jax.ShapeDtypeStruct
    return pl.pallas_call(
        _proj_kernel,
        grid=(nt, b),
        in_specs=[row(d),
                  pl.BlockSpec((1, 1, 3 * d), lambda i, bb: (bb, 0, 0)),
                  _const_spec(g_pre.shape), _const_spec(wa.shape),
                  pl.BlockSpec((tile, 6 * LANE), lambda i, bb: (i, 0)),
                  _const_spec(g_cq.shape), _const_spec(w_uq.shape), _const_spec(g_ckv.shape),
                  _const_spec(w_uk.shape), _const_spec(w_uv.shape)],
        out_specs=[row(256), row(512), pl.BlockSpec((1, MLA_HEADS, tile, LANE), lambda i, bb: (bb, 0, i, 0)), vt,
                   row(256), row(256), vt, row(512), row(128), row(128)],
        out_shape=[sds((b, n, 256), F32), sds((b, n, 512), BF16), sds((b, MLA_HEADS, n, LANE), BF16),
                   sds((b, nt, 256, tile), BF16), sds((b, n, 256), BF16), sds((b, n, 256), BF16),
                   sds((b, nt, 256, tile), BF16), sds((b, n, 512), BF16), sds((b, n, 128), BF16),
                   sds((b, n, 128), BF16)],
        compiler_params=_cparams(("arbitrary", "arbitrary")),
        name="proj",
    )(x, mod, g_pre, wa, tabs, g_cq, w_uq, g_ckv, w_uk, w_uv)


def _dense_kernel(*refs, kind, has_lat, lam_init, n_chunks, tk):
    refs = list(refs)
    q_ref = refs.pop(0)
    if has_lat:
        klat_ref, vlat_ref = refs.pop(0), refs.pop(0)
    kctx_ref, vctx_ref = refs.pop(0), refs.pop(0)
    if kind == "diff":
        lam_ref, gd_ref = refs.pop(0), refs.pop(0)
    o_ref = refs.pop(0)
    tq = q_ref.shape[1]
    if has_lat:
        vt_len = vlat_ref.shape[3]

    chains = []
    for j in range(2):
        if kind == "mla":
            chains.append((j, q_ref[0, :, LANE * j:LANE * (j + 1)]))
        else:
            q_all = q_ref[0]
            lane = lax.broadcasted_iota(jnp.int32, q_all.shape, 1)
            for m in range(2):
                lo = 2 * DIFF_QK * j + DIFF_QK * m
                chains.append((j, jnp.where((lane >= lo) & (lane < lo + DIFF_QK), q_all, jnp.zeros_like(q_all))))

    def keys(ref, j, rws):
        return ref[0, j, rws, :] if kind == "mla" else ref[0, rws, :]

    def score(kcs):
        ss = [lax.dot_general(kcs[j], qm, (((1,), (1,)), ((), ())), preferred_element_type=F32) for j, qm in chains]
        return ss, [jnp.max(s, axis=0, keepdims=True) for s in ss]

    def absorb(state, ss, maxes, vtcs):
        new = []
        for (m_run, acc), (j, _), s, mx in zip(state, chains, ss, maxes):
            m_new = jnp.maximum(m_run, mx)
            p = jnp.exp2(s - m_new).astype(BF16)
            new.append((m_new, jnp.exp2(m_run - m_new) * acc + jnp.dot(vtcs[j], p, preferred_element_type=F32)))
        return tuple(new)

    def ext(vt):
        return jnp.concatenate([vt, jnp.ones((ONES_ROWS, vt.shape[1]), BF16)], axis=0)

    def lat_keys(c):
        return [keys(klat_ref, j, slice(c * tk, (c + 1) * tk)) for j in range(2)]

    def lat_vals(c):
        pieces, k = [], c * tk
        while k < (c + 1) * tk:
            pi, off = divmod(k, vt_len)
            n = min((c + 1) * tk - k, vt_len - off)
            pieces.append((k - c * tk, n, pi, off))
            k += n
        return [[(rel, ext(vlat_ref[0, pi, HEAD_V * j:HEAD_V * (j + 1), off:off + n])) for rel, n, pi, off in pieces]
                for j in range(2)]

    state = tuple((jnp.full((1, tq), NEG, F32), jnp.zeros((HEAD_V + ONES_ROWS, tq), F32)) for _ in chains)
    ss_ctx, mx_ctx = score([keys(kctx_ref, j, slice(None)) for j in range(2)])
    vts_ctx = [ext(vctx_ref[0, 0, HEAD_V * j:HEAD_V * (j + 1), :]) for j in range(2)]
    if not has_lat:
        state = absorb(state, ss_ctx, mx_ctx, vts_ctx)
    else:
        def soften(ms, ss, maxes):
            new_ms, alphas, ps = [], [], []
            for m_run, s, mx in zip(ms, ss, maxes):
                m_new = jnp.maximum(m_run, mx)
                ps.append(jnp.exp2(s - m_new).astype(BF16))
                new_ms.append(m_new)
                alphas.append(jnp.exp2(m_run - m_new))
            return tuple(new_ms), alphas, ps

        def weigh(accs, alphas, ps, c):
            vts = lat_vals(c)
            out = []
            for acc, a, p, (j, _) in zip(accs, alphas, ps, chains):
                acc = a * acc
                for rel, vt in vts[j]:
                    acc = acc + jnp.dot(vt, p[rel:rel + vt.shape[1], :], preferred_element_type=F32)
                out.append(acc)
            return tuple(out)

        scored = {c: score(lat_keys(c)) for c in range(min(LOOKAHEAD + 1, n_chunks))}
        state = absorb(state, ss_ctx, mx_ctx, vts_ctx)
        ms, accs = tuple(m for m, _ in state), tuple(a for _, a in state)
        ms, al_prev, p_prev = soften(ms, *scored.pop(0))
        for c in range(1, n_chunks):
            ms, al_cur, p_cur = soften(ms, *scored.pop(c))
            if c + LOOKAHEAD < n_chunks:
                scored[c + LOOKAHEAD] = score(lat_keys(c + LOOKAHEAD))
            accs = weigh(accs, al_prev, p_prev, c - 1)
            al_prev, p_prev = al_cur, p_cur
        accs = weigh(accs, al_prev, p_prev, n_chunks - 1)
        state = tuple(zip(ms, accs))

    normed = [acc[:HEAD_V] / acc[HEAD_V:HEAD_V + 1] for _, acc in state]
    if kind == "mla":
        outs = normed
    else:
        lam_v = lam_ref[...]
        lam = (jnp.exp(jnp.sum(lam_v[0:1] * lam_v[1:2], axis=1, keepdims=True))
               - jnp.exp(jnp.sum(lam_v[2:3] * lam_v[3:4], axis=1, keepdims=True)) + lam_init)
        outs = []
        for j in range(2):
            o = normed[2 * j] - lam * normed[2 * j + 1]
            o = o * lax.rsqrt(jnp.sum(o * o, axis=0, keepdims=True) * (1.0 / DIFF_V) + EPS)
            outs.append(o * gd_ref[...] * (1.0 - lam_init))
    o_ref[0] = jnp.concatenate(outs, axis=0).T.astype(BF16)


def _dense_call(kind, q, k_lat, vt_lat, k_ctx, vt_ctx, extra, lam_init):
    b, n, _ = q.shape
    tq = min(Q_TILE[kind], n)
    assert n % tq == 0
    w = 2 * LANE if kind == "mla" else LANE
    has_lat = k_lat is not None
    c = vt_ctx.shape[3]

    def key_spec(rows):
        if kind == "mla":
            return pl.BlockSpec((1, 2, rows, LANE), lambda bb, hp, i: (bb, hp, 0, 0))
        return pl.BlockSpec((1, rows, LANE), lambda bb, hp, i: (bb, 0, hp))

    args, specs = [q], [pl.BlockSpec((1, tq, w), lambda bb, hp, i: (bb, i, hp))]
    n_chunks = tk = 0
    if has_lat:
        n_lat, vt_len = vt_lat.shape[1] * vt_lat.shape[3], vt_lat.shape[3]
        tk = KEY_CHUNK[kind]
        assert n_lat % tk == 0 and n_lat // tk >= 2
        n_chunks = n_lat // tk
        args += [k_lat, vt_lat]
        specs += [key_spec(n_lat), pl.BlockSpec((1, n_lat // vt_len, LANE, vt_len), lambda bb, hp, i: (bb, 0, hp, 0))]
    args += [k_ctx, vt_ctx]
    specs += [key_spec(c), pl.BlockSpec((1, 1, LANE, c), lambda bb, hp, i: (bb, 0, hp, 0))]
    for e in extra:
        args.append(e)
        specs.append(pl.BlockSpec(e.shape, lambda bb, hp, i: (0, 0)))
    return pl.pallas_call(
        functools.partial(_dense_kernel, kind=kind, has_lat=has_lat, lam_init=lam_init, n_chunks=n_chunks, tk=tk),
        grid=(b, 2, n // tq),
        in_specs=specs,
        out_specs=pl.BlockSpec((1, tq, LANE), lambda bb, hp, i: (bb, i, hp)),
        out_shape=jax.ShapeDtypeStruct((b, n, 2 * LANE), BF16),
        compiler_params=_cparams(("arbitrary", "arbitrary", "arbitrary")),
        name="dense_" + kind + ("_lat" if has_lat else "_ctx"),
    )(*args)


def _swa_kernel(*refs, has_win, n_total):
    refs = list(refs)
    q_ref = refs.pop(0)
    if has_win:
        kp_ref, km_ref, kn_ref, vp_ref, vm_ref, vn_ref = (refs.pop(0) for _ in range(6))
    kc_ref, vc_ref, sink_ref, o_ref = refs
    tq = q_ref.shape[1]
    nt = (((1,), (1,)), ((), ()))
    kc, vc = kc_ref[0], vc_ref[0]
    if has_win:
        t0 = pl.program_id(1) * tq
        kw = jnp.concatenate([kp_ref[0], km_ref[0], kn_ref[0]], axis=0)
        vw = jnp.concatenate([vp_ref[0], vm_ref[0], vn_ref[0]], axis=0)
        nk = kw.shape[0]
        qpos = t0 + lax.broadcasted_iota(jnp.int32, (tq, nk), 0)
        kpos = t0 - SWA_WINDOW + lax.broadcasted_iota(jnp.int32, (tq, nk), 1)
        dist = qpos - kpos
        visible = (dist <= SWA_WINDOW) & (dist >= -SWA_WINDOW) & (kpos >= 0) & (kpos < n_total)
    sink = sink_ref[...]
    res = []
    for j in range(SWA_HEADS):
        qj = q_ref[0, :, LANE * j:LANE * (j + 1)]
        sink_j = sink[:, j:j + 1]
        s_c = lax.dot_general(qj, kc, nt, preferred_element_type=F32)
        m = jnp.maximum(jnp.max(s_c, axis=1, keepdims=True), sink_j)
        if has_win:
            s_w = jnp.where(visible, lax.dot_general(qj, kw, nt, preferred_element_type=F32), NEG)
            m = jnp.maximum(m, jnp.max(s_w, axis=1, keepdims=True))
        p_c = jnp.exp(s_c - m)
        den = jnp.sum(p_c, axis=1, keepdims=True) + jnp.exp(sink_j - m)
        num = jnp.dot(p_c.astype(BF16), vc, preferred_element_type=F32)
        if has_win:
            p_w = jnp.exp(s_w - m)
            den = den + jnp.sum(p_w, axis=1, keepdims=True)
            num = num + jnp.dot(p_w.astype(BF16), vw, preferred_element_type=F32)
        res.append(num / den)
    lane = lax.broadcasted_iota(jnp.int32, (tq, LANE), 1)
    low = lane < SWA_HEAD
    o_ref[0, :, 0:LANE] = jnp.where(low, res[0], pltpu.roll(res[1], SWA_HEAD, 1)).astype(BF16)
    o_ref[0, :, LANE:2 * LANE] = jnp.where(low, pltpu.roll(res[2], SWA_HEAD, 1), res[3]).astype(BF16)


def _swa_call(q, k, v, k_ctx, v_ctx, sink, tile):
    b, n, _ = q.shape
    has_win = k is not None
    c = k_ctx.shape[1]
    args, specs = [q], [pl.BlockSpec((1, tile, 4 * LANE), lambda bb, i: (bb, i, 0))]
    if has_win:
        r = tile // SWA_WINDOW
        last = n // SWA_WINDOW - 1
        prev = pl.BlockSpec((1, SWA_WINDOW, LANE), lambda bb, i: (bb, jnp.maximum(i * r - 1, 0), 0))
        main = pl.BlockSpec((1, tile, LANE), lambda bb, i: (bb, i, 0))
        nxt = pl.BlockSpec((1, SWA_WINDOW, LANE), lambda bb, i: (bb, jnp.minimum((i + 1) * r, last), 0))
        args += [k, k, k, v, v, v]
        specs += [prev, main, nxt, prev, main, nxt]
    args += [k_ctx, v_ctx, sink]
    specs += [pl.BlockSpec((1, c, LANE), lambda bb, i: (bb, 0, 0)),
              pl.BlockSpec((1, c, LANE), lambda bb, i: (bb, 0, 0)),
              pl.BlockSpec(sink.shape, lambda bb, i: (0, 0))]
    return pl.pallas_call(
        functools.partial(_swa_kernel, has_win=has_win, n_total=n),
        grid=(b, n // tile),
        in_specs=specs,
        out_specs=pl.BlockSpec((1, tile, 2 * LANE), lambda bb, i: (bb, i, 0)),
        out_shape=jax.ShapeDtypeStruct((b, n, 2 * LANE), BF16),
        compiler_params=_cparams(("arbitrary", "arbitrary")),
        name="swa_win" if has_win else "swa_ctx",
    )(*args)


def _pool(u, u_prev, u_next, t0, n_total):
    tile = u.shape[0]
    ext = jnp.concatenate([jnp.where(t0 > 0, u_prev, 0.0), u, jnp.where(t0 + tile < n_total, u_next, 0.0)], axis=0)
    n_ext = ext.shape[0]

    def ahead(a, k):
        return pltpu.roll(a, n_ext - k, 0) if k else a

    sums, run, w = [], ext, 1
    for win in POOL_WINDOWS:
        while w < win:
            run = run + ahead(run, w)
            w *= 2
        sums.append(ahead(run, POOL_HALO - win // 2)[:tile])
    grp = lax.broadcasted_iota(jnp.int32, (tile, BRANCH_W), 1) // POOL_GROUP
    t = t0 + lax.broadcasted_iota(jnp.int32, (tile, BRANCH_W), 0)
    win_sum, half = sums[-1], jnp.full((tile, BRANCH_W), POOL_WINDOWS[-1] // 2, jnp.int32)
    for g in range(len(POOL_WINDOWS) - 2, -1, -1):
        win_sum = jnp.where(grp == g, sums[g], win_sum)
        half = jnp.where(grp == g, POOL_WINDOWS[g] // 2, half)
    lo = jnp.clip(t - half, 0, n_total)
    hi = jnp.clip(t + half, 0, n_total)
    return win_sum / (hi - lo).astype(F32) - u


def _merge_kernel(x_ref, mod_ref, gpre_ref, gpost_ref, wg_ref, wm_ref, up_ref, u_ref, un_ref, wpool_ref, spool_ref,
                  ymla_ref, ydiff_ref, yswa_ref, wbr_ref, wout_ref, o_ref, *, n_total):
    x = x_ref[0]
    d = x.shape[-1]
    tile = x.shape[0]
    mod = mod_ref[0]
    hb = _prenorm(x, mod, gpre_ref[...]).astype(BF16)

    pooled = _pool(u_ref[0], up_ref[0], un_ref[0], pl.program_id(0) * tile, n_total)
    y_pool = jnp.dot(pooled.astype(BF16), wpool_ref[...], preferred_element_type=F32) * spool_ref[...]
    ys = [y_pool, ymla_ref[0].astype(F32), ydiff_ref[0].astype(F32), yswa_ref[0].astype(F32)]

    merged = jnp.zeros((tile, d), F32)
    for r in range(N_BRANCH):
        g = jnp.dot(hb, wg_ref[:, BRANCH_W * r:BRANCH_W * (r + 1)], preferred_element_type=F32)
        g = g * _sigmoid(g)
        mg = _sigmoid(jnp.dot(hb, wm_ref[:, d * r:d * (r + 1)], preferred_element_type=F32))
        merged = merged + mg * jnp.dot((ys[r] * g).astype(BF16), wbr_ref[r], preferred_element_type=F32)
    out = jnp.dot(merged.astype(BF16), wout_ref[...], preferred_element_type=F32)
    o_ref[0] = x + mod[:, 2 * d:] * _rms(out, gpost_ref[...], d)


def _merge_call(x, mod, g_pre, g_post, wg, wm, u, wpool, s_pool, y_mla, y_diff, y_swa, w_br, w_out, tile):
    b, n, d = x.shape
    r = tile // POOL_HALO
    last = n // POOL_HALO - 1
    row = lambda w: pl.BlockSpec((1, tile, w), lambda i, bb: (bb, i, 0))
    return pl.pallas_call(
        functools.partial(_merge_kernel, n_total=n),
        grid=(n // tile, b),
        in_specs=[row(d),
                  pl.BlockSpec((1, 1, 3 * d), lambda i, bb: (bb, 0, 0)),
                  _const_spec(g_pre.shape), _const_spec(g_post.shape), _const_spec(wg.shape), _const_spec(wm.shape),
                  pl.BlockSpec((1, POOL_HALO, BRANCH_W), lambda i, bb: (bb, jnp.maximum(i * r - 1, 0), 0)),
                  row(BRANCH_W),
                  pl.BlockSpec((1, POOL_HALO, BRANCH_W), lambda i, bb: (bb, jnp.minimum((i + 1) * r, last), 0)),
                  _const_spec(wpool.shape), _const_spec(s_pool.shape),
                  row(BRANCH_W), row(BRANCH_W), row(BRANCH_W),
                  _const_spec(w_br.shape), _const_spec(w_out.shape)],
        out_specs=row(d),
        out_shape=jax.ShapeDtypeStruct((b, n, d), F32),
        compiler_params=_cparams(("arbitrary", "arbitrary")),
        name="merge",
    )(x, mod, g_pre, g_post, wg, wm, u, u, u, wpool, s_pool, y_mla, y_diff, y_swa, w_br, w_out)


def _rope_tables(n_tok, identity):
    if identity:
        one, zero = jnp.ones((n_tok, LANE), F32), jnp.zeros((n_tok, LANE), F32)
        return jnp.concatenate([one, zero, one, zero, one, zero], axis=1)
    t = jnp.arange(n_tok, dtype=jnp.int32)
    row = (t // GRID_W).astype(F32)[:, None]
    col = (t % GRID_W).astype(F32)[:, None]

    def cs(rot_dim):
        n_freq = rot_dim // 4
        inv = jnp.exp(-math.log(ROPE_BASE) * jnp.arange(n_freq, dtype=F32) / n_freq)
        ang = jnp.concatenate([row * inv, col * inv], axis=-1)
        c, s = jnp.cos(ang), jnp.sin(ang)
        return jnp.concatenate([c, c], axis=1), jnp.concatenate([-s, s], axis=1)

    c32, s32 = cs(DIFF_QK)
    c64, s64 = cs(SWA_HEAD)
    one, zero = jnp.ones((n_tok, MLA_NOPE), F32), jnp.zeros((n_tok, MLA_NOPE), F32)
    cm = jnp.concatenate([one, c32, one[:, :MLA_ROPE]], axis=1)
    sm = jnp.concatenate([zero, s32, zero[:, :MLA_ROPE]], axis=1)
    return jnp.concatenate([jnp.tile(c32, (1, 4)), jnp.tile(s32, (1, 4)), jnp.tile(c64, (1, 2)),
                            jnp.tile(s64, (1, 2)), cm, sm], axis=1)


def _pack_layer(w_in, w_pool, g_cq, w_uq, w_uk, w_br, w_out):
    d = w_in.shape[0]
    col = lambda name: w_in[:, _OFF[name][0]:_OFF[name][1]]
    z = lambda w: jnp.zeros((d, w), w_in.dtype)
    swa_q = col('swa_q')
    swa_blocks = []
    for j in range(SWA_HEADS):
        qj = swa_q[:, SWA_HEAD * j:SWA_HEAD * (j + 1)]
        swa_blocks += [qj, z(SWA_HEAD)] if j // 2 == 0 else [z(SWA_HEAD), qj]
    wa = jnp.concatenate(
        [col('pool_in'), col('mla_cq'), z(256 - MLA_Q_RANK), col('mla_ckv'),
         z(MLA_NOPE), col('mla_kr'), z(LANE - MLA_NOPE - MLA_ROPE),
         col('diff_q'), col('diff_k'), col('diff_v')] + swa_blocks + [col('swa_k'), col('swa_v')], axis=1)
    hq = MLA_NOPE + MLA_ROPE
    uq = jnp.pad(w_uq.reshape(MLA_Q_RANK, MLA_HEADS, hq), ((0, 256 - MLA_Q_RANK), (0, 0), (0, LANE - hq)))
    uk = jnp.pad(w_uk.reshape(MLA_KV_RANK, MLA_HEADS, MLA_NOPE), ((0, 0), (0, 0), (0, LANE - MLA_NOPE)))
    wpool = jnp.zeros((BRANCH_W, BRANCH_W), w_pool.dtype)
    for g in range(len(POOL_WINDOWS)):
        sl = slice(POOL_GROUP * g, POOL_GROUP * (g + 1))
        wpool = wpool.at[sl, sl].set(w_pool[g])
    return dict(
        wa=wa.astype(BF16), wg=col('gates').astype(BF16), wm=col('merge').astype(BF16),
        g_cq=jnp.pad(g_cq, (0, 256 - MLA_Q_RANK)).reshape(1, 256),
        w_uq=uq.reshape(256, MLA_HEADS * LANE).astype(BF16), w_uk=uk.reshape(MLA_KV_RANK, MLA_HEADS * LANE).astype(BF16),
        wpool=wpool.astype(BF16), w_br=w_br.astype(BF16), w_out=w_out.astype(BF16))


def kernel(x, c, ctx, c_ctx, w_mod, b_mod, g_pre, g_post, w_in, w_pool, s_pool, g_cq, w_uq, g_ckv, w_uk, w_uv,
           lam_q1, lam_k1, lam_q2, lam_k2, g_diff, sink, w_br, w_out):
    b, n, d = x.shape
    n_ctx = ctx.shape[1]
    depth = w_in.shape[0]
    tile = min(SEQ_TILE, n)
    assert n % tile == 0 and n_ctx % LANE == 0 and n_ctx <= SEQ_TILE and d % LANE == 0

    rows = -(-(b + 1) // 8) * 8
    cc = jnp.zeros((rows, d), F32).at[:b].set(c).at[b].set(c_ctx)
    mod_all = _mod_call(cc, w_mod, b_mod)
    tabs = _rope_tables(n, False)
    tabs_ctx = _rope_tables(n_ctx, True)

    xc = ctx
    for l in range(depth):
        need_ctx = l < depth - 1
        lam_init = 0.8 - 0.6 * math.exp(-0.3 * l)
        p = _pack_layer(w_in[l], w_pool[l], g_cq[l], w_uq[l], w_uk[l], w_br[l], w_out[l])
        mod = mod_all[l, :b].reshape(b, 1, 3 * d)
        mod_c = jnp.broadcast_to(mod_all[l, b].reshape(1, 1, 3 * d), (b, 1, 3 * d))
        gp, gpo = g_pre[l].reshape(1, d), g_post[l].reshape(1, d)
        gckv = g_ckv[l].reshape(1, MLA_KV_RANK)
        wuv = w_uv[l].astype(BF16)
        lam_v = jnp.stack([lam_q1[l], lam_k1[l], lam_q2[l], lam_k2[l]])
        gd = g_diff[l].reshape(DIFF_V, 1)
        sk = sink[l].reshape(1, SWA_HEADS)
        sp = s_pool[l].reshape(1, BRANCH_W)

        proj = lambda xx, mm, tt, tl: _proj_call(xx, mm, gp, p['wa'], tt, p['g_cq'], p['w_uq'], gckv, p['w_uk'],
                                                 wuv, tl)
        u, mq, mk, mvt, dq, dk, dvt, sq, skk, sv = proj(x, mod, tabs, tile)
        uc, mqc, mkc, mvtc, dqc, dkc, dvtc, sqc, skc, svc = proj(xc, mod_c, tabs_ctx, n_ctx)

        y_mla = _dense_call("mla", mq, mk, mvt, mkc, mvtc, (), lam_init)
        y_diff = _dense_call("diff", dq, dk, dvt, dkc, dvtc, (lam_v, gd), lam_init)
        y_swa = _swa_call(sq, skk, sv, skc, svc, sk, min(SWA_TILE, n))
        merge = lambda xx, mm, uu, ya, yb, yc, tl: _merge_call(xx, mm, gp, gpo, p['wg'], p['wm'], uu, p['wpool'], sp,
                                                              ya, yb, yc, p['w_br'], p['w_out'], tl)
        x_new = merge(x, mod, u, y_mla, y_diff, y_swa, tile)
        if need_ctx:
            yc_mla = _dense_call("mla", mqc, None, None, mkc, mvtc, (), lam_init)
            yc_diff = _dense_call("diff", dqc, None, None, dkc, dvtc, (lam_v, gd), lam_init)
            yc_swa = _swa_call(sqc, None, None, skc, svc, sk, n_ctx)
            xc = merge(xc, mod_c, uc, yc_mla, yc_diff, yc_swa, n_ctx)
        x = x_new
    return x
```

```python
import functools
import math

import jax
import jax.numpy as jnp
from jax import lax
from jax.experimental import pallas as pl
from jax.experimental.pallas import tpu as pltpu

F32 = jnp.float32
BF16 = jnp.bfloat16

GRID_W = 64
EPS = 1e-6
ROPE_BASE = 10000.0
NEG = -1e30
N_BRANCH = 4
BRANCH_W = 256
POOL_WINDOWS = (2, 4, 8, 16)
POOL_GROUP = BRANCH_W // len(POOL_WINDOWS)
POOL_HALO = 8
MLA_HEADS, MLA_NOPE, MLA_ROPE, MLA_V = 4, 64, 32, 64
MLA_Q_RANK, MLA_KV_RANK = 192, 128
DIFF_HEADS, DIFF_QK, DIFF_V = 4, 32, 64
SWA_HEADS, SWA_KV_HEADS, SWA_HEAD, SWA_WINDOW = 4, 2, 64, 128
HEAD_V = 64
LOG2E = math.log2(math.e)

LANE = 128
ONES_ROWS = 16
VMEM_LIMIT = 56 * 1024 * 1024
SEQ_TILE = 512
SWA_TILE = 512
EXP2_GUARD = 60.0
KEY_CHUNK = {"mla": 256, "diff": 512}
Q_TILE = {"mla": 512, "diff": 256}

_SPLITS = (('pool_in', 256), ('mla_cq', 192), ('mla_ckv', 128), ('mla_kr', 32), ('diff_q', 256), ('diff_k', 256),
           ('diff_v', 256), ('swa_q', 256), ('swa_k', 128), ('swa_v', 128), ('gates', 1024), ('merge', 4096))
_OFF = {}
_o = 0
for _n, _w in _SPLITS:
    _OFF[_n] = (_o, _o + _w)
    _o += _w

A_POOL, A_CQ, A_CKV, A_KR, A_DQ, A_DK, A_DV, A_SQ, A_SK, A_SV, A_END = (
    0, 256, 512, 640, 768, 1024, 1280, 1536, 2048, 2176, 2304)


def _cparams(sem):
    return pltpu.CompilerParams(dimension_semantics=sem, vmem_limit_bytes=VMEM_LIMIT)


def _const_spec(shape):
    nd = len(shape)
    return pl.BlockSpec(shape, lambda *_: (0,) * nd, pipeline_mode=pl.Buffered(1))


def _sigmoid(x):
    return 0.5 * jnp.tanh(0.5 * x) + 0.5


def _rms(x, g, n):
    return x * lax.rsqrt(jnp.sum(x * x, axis=-1, keepdims=True) * (1.0 / n) + EPS) * g


def _prenorm(x, mod, g_pre):
    d = x.shape[-1]
    shift, scale = mod[:, :d], mod[:, d:2 * d]
    return _rms(x, g_pre, d) * (1.0 + scale) + shift


def _rope128(x, c, s, period, half, first_below):
    lane = lax.broadcasted_iota(jnp.int32, x.shape, 1)
    first = (lane % period) < first_below
    swapped = jnp.where(first, pltpu.roll(x, LANE - half, 1), pltpu.roll(x, half, 1))
    return x * c + swapped * s


def _rope_mla(x, c, s):
    return _rope128(x, c, s, LANE, MLA_ROPE // 2, MLA_NOPE + MLA_ROPE // 2)


def _rope_heads(x, c, s, head_dim):
    return _rope128(x, c, s, head_dim, head_dim // 2, head_dim // 2)


def _mod_kernel(c_ref, w_ref, b_ref, o_ref):
    a = c_ref[...]
    a = a * jax.nn.sigmoid(a)
    w = w_ref[0]
    a_hi = a.astype(BF16)
    a_lo = (a - a_hi.astype(F32)).astype(BF16)
    w_hi = w.astype(BF16)
    w_lo = (w - w_hi.astype(F32)).astype(BF16)
    acc = jnp.dot(a_hi, w_hi, preferred_element_type=F32)
    acc += jnp.dot(a_lo, w_hi, preferred_element_type=F32)
    acc += jnp.dot(a_hi, w_lo, preferred_element_type=F32)
    o_ref[0] = acc + b_ref[0]


def _mod_call(cc, w_mod, b_mod):
    n_layer, d, d3 = w_mod.shape
    rows = cc.shape[0]
    return pl.pallas_call(
        _mod_kernel,
        grid=(n_layer, d3 // d),
        in_specs=[pl.BlockSpec((rows, d), lambda l, j: (0, 0)),
                  pl.BlockSpec((1, d, d), lambda l, j: (l, 0, j)),
                  pl.BlockSpec((1, 1, d), lambda l, j: (l, 0, j))],
        out_specs=pl.BlockSpec((1, rows, d), lambda l, j: (l, 0, j)),
        out_shape=jax.ShapeDtypeStruct((n_layer, rows, d3), F32),
        compiler_params=_cparams(("arbitrary", "arbitrary")),
        name="mod",
    )(cc, w_mod, b_mod.reshape(n_layer, 1, d3))


def _proj_kernel(x_ref, mod_ref, gpre_ref, wa_ref, tabs_ref, gcq_ref, wuq_ref, gckv_ref, wuk_ref, wuv_ref,
                 u_ref, mq_ref, mk_ref, mvt_ref, dq_ref, dk_ref, dvt_ref, sq_ref, sk_ref, sv_ref):
    hb = _prenorm(x_ref[0], mod_ref[0], gpre_ref[...]).astype(BF16)

    def sect(lo, hi):
        return jnp.dot(hb, wa_ref[:, lo:hi], preferred_element_type=F32)

    tabs = tabs_ref[...]
    c32, s32, c64, s64, cm, sm = (tabs[:, LANE * j:LANE * (j + 1)] for j in range(6))

    u_ref[0] = sect(A_POOL, A_CQ)

    cqn = _rms(sect(A_CQ, A_CKV), gcq_ref[...], MLA_Q_RANK).astype(BF16)
    q = jnp.dot(cqn, wuq_ref[...], preferred_element_type=F32)
    mla_scale = (MLA_NOPE + MLA_ROPE) ** -0.5 * LOG2E
    for h in range(MLA_HEADS):
        blk = _rope_mla(q[:, LANE * h:LANE * (h + 1)], cm, sm)
        mq_ref[0, :, LANE * h:LANE * (h + 1)] = (blk * mla_scale).astype(BF16)
    ckvn = _rms(sect(A_CKV, A_KR), gckv_ref[...], MLA_KV_RANK).astype(BF16)
    k_nope = jnp.dot(ckvn, wuk_ref[...], preferred_element_type=F32)
    k_rope = sect(A_KR, A_DQ)
    for h in range(MLA_HEADS):
        blk = _rope_mla(k_nope[:, LANE * h:LANE * (h + 1)] + k_rope, cm, sm)
        mk_ref[0, h] = blk.astype(BF16)
    mvt_ref[0, 0] = jnp.dot(ckvn, wuv_ref[...], preferred_element_type=F32).T.astype(BF16)

    dq = sect(A_DQ, A_DK)
    dk = sect(A_DK, A_DV)
    diff_scale = DIFF_QK ** -0.5 * LOG2E
    for j in range(2):
        sl = slice(LANE * j, LANE * (j + 1))
        dq_ref[0, :, sl] = (_rope_heads(dq[:, sl], c32, s32, DIFF_QK) * diff_scale).astype(BF16)
        dk_ref[0, :, sl] = _rope_heads(dk[:, sl], c32, s32, DIFF_QK).astype(BF16)
    dvt_ref[0, 0] = sect(A_DV, A_SQ).T.astype(BF16)

    sq = sect(A_SQ, A_SK)
    swa_scale = SWA_HEAD ** -0.5
    for j in range(SWA_HEADS):
        sl = slice(LANE * j, LANE * (j + 1))
        sq_ref[0, :, sl] = (_rope_heads(sq[:, sl], c64, s64, SWA_HEAD) * swa_scale).astype(BF16)
    sk_ref[0] = _rope_heads(sect(A_SK, A_SV), c64, s64, SWA_HEAD).astype(BF16)
    sv_ref[0] = sect(A_SV, A_END).astype(BF16)


def _proj_call(x, mod, g_pre, wa, tabs, g_cq, w_uq, g_ckv, w_uk, w_uv, tile):
    b, n, d = x.shape
    nt = n // tile
    row = lambda w: pl.BlockSpec((1, tile, w), lambda i, bb: (bb, i, 0))
    vt = pl.BlockSpec((1, 1, 2 * LANE, tile), lambda i, bb: (bb, i, 0, 0))
    sds = jax.ShapeDtypeStruct
    return pl.pallas_call(
        _proj_kernel,
        grid=(nt, b),
        in_specs=[row(d),
                  pl.BlockSpec((1, 1, 3 * d), lambda i, bb: (bb, 0, 0)),
                  _const_spec(g_pre.shape), _const_spec(wa.shape),
                  pl.BlockSpec((tile, 6 * LANE), lambda i, bb: (i, 0)),
                  _const_spec(g_cq.shape), _const_spec(w_uq.shape), _const_spec(g_ckv.shape),
                  _const_spec(w_uk.shape), _const_spec(w_uv.shape)],
        out_specs=[row(256), row(512), pl.BlockSpec((1, MLA_HEADS, tile, LANE), lambda i, bb: (bb, 0, i, 0)), vt,
                   row(256), row(256), vt, row(512), row(128), row(128)],
        out_shape=[sds((b, n, 256), F32), sds((b, n, 512), BF16), sds((b, MLA_HEADS, n, LANE), BF16),
                   sds((b, nt, 256, tile), BF16), sds((b, n, 256), BF16), sds((b, n, 256), BF16),
                   sds((b, nt, 256, tile), BF16), sds((b, n, 512), BF16), sds((b, n, 128), BF16),
                   sds((b, n, 128), BF16)],
        compiler_params=_cparams(("arbitrary", "arbitrary")),
        name="proj",
    )(x, mod, g_pre, wa, tabs, g_cq, w_uq, g_ckv, w_uk, w_uv)


def _dense_kernel(*refs, kind, has_lat, lam_init, n_chunks, tk):
    refs = list(refs)
    q_ref = refs.pop(0)
    if has_lat:
        klat_ref, vlat_ref = refs.pop(0), refs.pop(0)
    kctx_ref, vctx_ref = refs.pop(0), refs.pop(0)
    if kind == "diff":
        lam_ref, gd_ref = refs.pop(0), refs.pop(0)
    o_ref = refs.pop(0)
    tq = q_ref.shape[1]
    if has_lat:
        vt_len = vlat_ref.shape[3]

    chains = []
    for j in range(2):
        if kind == "mla":
            chains.append((j, q_ref[0, :, LANE * j:LANE * (j + 1)]))
        else:
            q_all = q_ref[0]
            lane = lax.broadcasted_iota(jnp.int32, q_all.shape, 1)
            for m in range(2):
                lo = 2 * DIFF_QK * j + DIFF_QK * m
                chains.append((j, jnp.where((lane >= lo) & (lane < lo + DIFF_QK), q_all, jnp.zeros_like(q_all))))

    def keys(ref, j, rws):
        return ref[0, j, rws, :] if kind == "mla" else ref[0, rws, :]

    def score(kcs):
        ss = [lax.dot_general(kcs[j], qm, (((1,), (1,)), ((), ())), preferred_element_type=F32) for j, qm in chains]
        return ss, [jnp.max(s, axis=0, keepdims=True) for s in ss]

    def absorb(state, ss, maxes, vtcs):
        new = []
        for (m_run, acc), (j, _), s, mx in zip(state, chains, ss, maxes):
            m_new = jnp.maximum(m_run, mx)
            p = jnp.exp2(s - m_new).astype(BF16)
            new.append((m_new, jnp.exp2(m_run - m_new) * acc + jnp.dot(vtcs[j], p, preferred_element_type=F32)))
        return tuple(new)

    def ext(vt):
        return jnp.concatenate([vt, jnp.ones((ONES_ROWS, vt.shape[1]), BF16)], axis=0)

    def lat_keys(c):
        return [keys(klat_ref, j, slice(c * tk, (c + 1) * tk)) for j in range(2)]

    def lat_vals(c):
        pieces, k = [], c * tk
        while k < (c + 1) * tk:
            pi, off = divmod(k, vt_len)
            n = min((c + 1) * tk - k, vt_len - off)
            pieces.append((k - c * tk, n, pi, off))
            k += n
        return [[(rel, ext(vlat_ref[0, pi, HEAD_V * j:HEAD_V * (j + 1), off:off + n])) for rel, n, pi, off in pieces]
                for j in range(2)]

    def weigh(accs, alphas, ps, c):
        vts = lat_vals(c)
        out = []
        for acc, a, p, (j, _) in zip(accs, alphas, ps, chains):
            acc = acc if a is None else a * acc
            for rel, vt in vts[j]:
                acc = acc + jnp.dot(vt, p[rel:rel + vt.shape[1], :], preferred_element_type=F32)
            out.append(acc)
        return tuple(out)

    def context_state():
        state = tuple((jnp.full((1, tq), NEG, F32), jnp.zeros((HEAD_V + ONES_ROWS, tq), F32)) for _ in chains)
        ss, maxes = score([keys(kctx_ref, j, slice(None)) for j in range(2)])
        return absorb(state, ss, maxes, [ext(vctx_ref[0, 0, HEAD_V * j:HEAD_V * (j + 1), :]) for j in range(2)])

    def exact_path():
        def soften(ms, ss, maxes):
            new_ms, alphas, ps = [], [], []
            for m_run, s, mx in zip(ms, ss, maxes):
                m_new = jnp.maximum(m_run, mx)
                ps.append(jnp.exp2(s - m_new).astype(BF16))
                new_ms.append(m_new)
                alphas.append(jnp.exp2(m_run - m_new))
            return tuple(new_ms), alphas, ps

        scored = {c: score(lat_keys(c)) for c in range(2)}
        state = context_state()
        ms, accs = tuple(m for m, _ in state), tuple(a for _, a in state)
        ms, al_prev, p_prev = soften(ms, *scored.pop(0))
        for c in range(1, n_chunks):
            ms, al_cur, p_cur = soften(ms, *scored.pop(c))
            if c + 1 < n_chunks:
                scored[c + 1] = score(lat_keys(c + 1))
            accs = weigh(accs, al_prev, p_prev, c - 1)
            al_prev, p_prev = al_cur, p_cur
        return weigh(accs, al_prev, p_prev, n_chunks - 1)

    def lagged_path():
        def soften(refs_, ss):
            ps, new_refs, alphas, over = [], [], [], []
            for r, s in zip(refs_, ss):
                mx = jnp.max(s, axis=0, keepdims=True)
                ps.append(jnp.exp2(s - r).astype(BF16))
                r_new = jnp.maximum(r, mx)
                new_refs.append(r_new)
                alphas.append(jnp.exp2(r - r_new))
                over.append(mx - r)
            return ps, new_refs, alphas, over

        def dots(c):
            kcs = lat_keys(c)
            return [lax.dot_general(kcs[j], qm, (((1,), (1,)), ((), ())), preferred_element_type=F32)
                    for j, qm in chains]

        ss = dots(0)
        state = context_state()
        refs_, accs = [m for m, _ in state], tuple(a for _, a in state)
        p_cur, refs_, al_next, excess = soften(refs_, ss)
        al_cur = [None] * len(chains)
        for c in range(n_chunks):
            if c + 1 < n_chunks:
                p_nxt, refs_, al_after, over = soften(refs_, dots(c + 1))
                excess = [jnp.maximum(e, o) for e, o in zip(excess, over)]
            accs = weigh(accs, al_cur, p_cur, c)
            if c + 1 < n_chunks:
                p_cur, al_cur, al_next = p_nxt, al_next, al_after
        worst = excess[0]
        for e in excess[1:]:
            worst = jnp.maximum(worst, e)
        return accs, jnp.max(worst)

    def finish(accs):
        normed = [acc[:HEAD_V] / acc[HEAD_V:HEAD_V + 1] for acc in accs]
        if kind == "mla":
            outs = normed
        else:
            lam_v = lam_ref[...]
            lam = (jnp.exp(jnp.sum(lam_v[0:1] * lam_v[1:2], axis=1, keepdims=True))
                   - jnp.exp(jnp.sum(lam_v[2:3] * lam_v[3:4], axis=1, keepdims=True)) + lam_init)
            outs = []
            for j in range(2):
                o = normed[2 * j] - lam * normed[2 * j + 1]
                o = o * lax.rsqrt(jnp.sum(o * o, axis=0, keepdims=True) * (1.0 / DIFF_V) + EPS)
                outs.append(o * gd_ref[...] * (1.0 - lam_init))
        o_ref[0] = jnp.concatenate(outs, axis=0).T.astype(BF16)

    if not has_lat:
        finish([acc for _, acc in context_state()])
    else:
        accs, worst = lagged_path()
        finish(accs)

        @pl.when(worst > EXP2_GUARD)
        def _():
            finish(exact_path())


def _dense_call(kind, q, k_lat, vt_lat, k_ctx, vt_ctx, extra, lam_init):
    b, n, _ = q.shape
    tq = min(Q_TILE[kind], n)
    assert n % tq == 0
    w = 2 * LANE if kind == "mla" else LANE
    has_lat = k_lat is not None
    c = vt_ctx.shape[3]

    def key_spec(rows):
        if kind == "mla":
            return pl.BlockSpec((1, 2, rows, LANE), lambda bb, hp, i: (bb, hp, 0, 0))
        return pl.BlockSpec((1, rows, LANE), lambda bb, hp, i: (bb, 0, hp))

    args, specs = [q], [pl.BlockSpec((1, tq, w), lambda bb, hp, i: (bb, i, hp))]
    n_chunks = tk = 0
    if has_lat:
        n_lat, vt_len = vt_lat.shape[1] * vt_lat.shape[3], vt_lat.shape[3]
        tk = KEY_CHUNK[kind]
        assert n_lat % tk == 0 and n_lat // tk >= 2
        n_chunks = n_lat // tk
        args += [k_lat, vt_lat]
        specs += [key_spec(n_lat), pl.BlockSpec((1, n_lat // vt_len, LANE, vt_len), lambda bb, hp, i: (bb, 0, hp, 0))]
    args += [k_ctx, vt_ctx]
    specs += [key_spec(c), pl.BlockSpec((1, 1, LANE, c), lambda bb, hp, i: (bb, 0, hp, 0))]
    for e in extra:
        args.append(e)
        specs.append(pl.BlockSpec(e.shape, lambda bb, hp, i: (0, 0)))
    return pl.pallas_call(
        functools.partial(_dense_kernel, kind=kind, has_lat=has_lat, lam_init=lam_init, n_chunks=n_chunks, tk=tk),
        grid=(b, 2, n // tq),
        in_specs=specs,
        out_specs=pl.BlockSpec((1, tq, LANE), lambda bb, hp, i: (bb, i, hp)),
        out_shape=jax.ShapeDtypeStruct((b, n, 2 * LANE), BF16),
        compiler_params=_cparams(("arbitrary", "arbitrary", "arbitrary")),
        name="dense_" + kind + ("_lat" if has_lat else "_ctx"),
    )(*args)


def _swa_kernel(*refs, has_win, n_total):
    refs = list(refs)
    q_ref = refs.pop(0)
    if has_win:
        kp_ref, km_ref, kn_ref, vp_ref, vm_ref, vn_ref = (refs.pop(0) for _ in range(6))
    kc_ref, vc_ref, sink_ref, o_ref = refs
    tq = q_ref.shape[1]
    nt = (((1,), (1,)), ((), ()))
    kc, vc = kc_ref[0], vc_ref[0]
    if has_win:
        t0 = pl.program_id(1) * tq
        kw = jnp.concatenate([kp_ref[0], km_ref[0], kn_ref[0]], axis=0)
        vw = jnp.concatenate([vp_ref[0], vm_ref[0], vn_ref[0]], axis=0)
        nk = kw.shape[0]
        qpos = t0 + lax.broadcasted_iota(jnp.int32, (tq, nk), 0)
        kpos = t0 - SWA_WINDOW + lax.broadcasted_iota(jnp.int32, (tq, nk), 1)
        dist = qpos - kpos
        visible = (dist <= SWA_WINDOW) & (dist >= -SWA_WINDOW) & (kpos >= 0) & (kpos < n_total)
    sink = sink_ref[...]
    res = []
    for j in range(SWA_HEADS):
        qj = q_ref[0, :, LANE * j:LANE * (j + 1)]
        sink_j = sink[:, j:j + 1]
        s_c = lax.dot_general(qj, kc, nt, preferred_element_type=F32)
        m = jnp.maximum(jnp.max(s_c, axis=1, keepdims=True), sink_j)
        if has_win:
            s_w = jnp.where(visible, lax.dot_general(qj, kw, nt, preferred_element_type=F32), NEG)
            m = jnp.maximum(m, jnp.max(s_w, axis=1, keepdims=True))
        p_c = jnp.exp(s_c - m)
        den = jnp.sum(p_c, axis=1, keepdims=True) + jnp.exp(sink_j - m)
        num = jnp.dot(p_c.astype(BF16), vc, preferred_element_type=F32)
        if has_win:
            p_w = jnp.exp(s_w - m)
            den = den + jnp.sum(p_w, axis=1, keepdims=True)
            num = num + jnp.dot(p_w.astype(BF16), vw, preferred_element_type=F32)
        res.append(num / den)
    lane = lax.broadcasted_iota(jnp.int32, (tq, LANE), 1)
    low = lane < SWA_HEAD
    o_ref[0, :, 0:LANE] = jnp.where(low, res[0], pltpu.roll(res[1], SWA_HEAD, 1)).astype(BF16)
    o_ref[0, :, LANE:2 * LANE] = jnp.where(low, pltpu.roll(res[2], SWA_HEAD, 1), res[3]).astype(BF16)


def _swa_call(q, k, v, k_ctx, v_ctx, sink, tile):
    b, n, _ = q.shape
    has_win = k is not None
    c = k_ctx.shape[1]
    args, specs = [q], [pl.BlockSpec((1, tile, 4 * LANE), lambda bb, i: (bb, i, 0))]
    if has_win:
        r = tile // SWA_WINDOW
        last = n // SWA_WINDOW - 1
        prev = pl.BlockSpec((1, SWA_WINDOW, LANE), lambda bb, i: (bb, jnp.maximum(i * r - 1, 0), 0))
        main = pl.BlockSpec((1, tile, LANE), lambda bb, i: (bb, i, 0))
        nxt = pl.BlockSpec((1, SWA_WINDOW, LANE), lambda bb, i: (bb, jnp.minimum((i + 1) * r, last), 0))
        args += [k, k, k, v, v, v]
        specs += [prev, main, nxt, prev, main, nxt]
    args += [k_ctx, v_ctx, sink]
    specs += [pl.BlockSpec((1, c, LANE), lambda bb, i: (bb, 0, 0)),
              pl.BlockSpec((1, c, LANE), lambda bb, i: (bb, 0, 0)),
              pl.BlockSpec(sink.shape, lambda bb, i: (0, 0))]
    return pl.pallas_call(
        functools.partial(_swa_kernel, has_win=has_win, n_total=n),
        grid=(b, n // tile),
        in_specs=specs,
        out_specs=pl.BlockSpec((1, tile, 2 * LANE), lambda bb, i: (bb, i, 0)),
        out_shape=jax.ShapeDtypeStruct((b, n, 2 * LANE), BF16),
        compiler_params=_cparams(("arbitrary", "arbitrary")),
        name="swa_win" if has_win else "swa_ctx",
    )(*args)


def _pool(u, u_prev, u_next, t0, n_total):
    tile = u.shape[0]
    ext = jnp.concatenate([jnp.where(t0 > 0, u_prev, 0.0), u, jnp.where(t0 + tile < n_total, u_next, 0.0)], axis=0)
    n_ext = ext.shape[0]

    def ahead(a, k):
        return pltpu.roll(a, n_ext - k, 0) if k else a

    sums, run, w = [], ext, 1
    for win in POOL_WINDOWS:
        while w < win:
            run = run + ahead(run, w)
            w *= 2
        sums.append(ahead(run, POOL_HALO - win // 2)[:tile])
    grp = lax.broadcasted_iota(jnp.int32, (tile, BRANCH_W), 1) // POOL_GROUP
    t = t0 + lax.broadcasted_iota(jnp.int32, (tile, BRANCH_W), 0)
    win_sum, half = sums[-1], jnp.full((tile, BRANCH_W), POOL_WINDOWS[-1] // 2, jnp.int32)
    for g in range(len(POOL_WINDOWS) - 2, -1, -1):
        win_sum = jnp.where(grp == g, sums[g], win_sum)
        half = jnp.where(grp == g, POOL_WINDOWS[g] // 2, half)
    lo = jnp.clip(t - half, 0, n_total)
    hi = jnp.clip(t + half, 0, n_total)
    return win_sum / (hi - lo).astype(F32) - u


def _merge_kernel(x_ref, mod_ref, gpre_ref, gpost_ref, wg_ref, wm_ref, up_ref, u_ref, un_ref, wpool_ref, spool_ref,
                  ymla_ref, ydiff_ref, yswa_ref, wbr_ref, wout_ref, o_ref, *, n_total):
    x = x_ref[0]
    d = x.shape[-1]
    tile = x.shape[0]
    mod = mod_ref[0]
    hb = _prenorm(x, mod, gpre_ref[...]).astype(BF16)

    pooled = _pool(u_ref[0], up_ref[0], un_ref[0], pl.program_id(0) * tile, n_total)
    y_pool = jnp.dot(pooled.astype(BF16), wpool_ref[...], preferred_element_type=F32) * spool_ref[...]
    ys = [y_pool, ymla_ref[0].astype(F32), ydiff_ref[0].astype(F32), yswa_ref[0].astype(F32)]

    merged = jnp.zeros((tile, d), F32)
    for r in range(N_BRANCH):
        g = jnp.dot(hb, wg_ref[:, BRANCH_W * r:BRANCH_W * (r + 1)], preferred_element_type=F32)
        g = g * _sigmoid(g)
        mg = _sigmoid(jnp.dot(hb, wm_ref[:, d * r:d * (r + 1)], preferred_element_type=F32))
        merged = merged + mg * jnp.dot((ys[r] * g).astype(BF16), wbr_ref[r], preferred_element_type=F32)
    out = jnp.dot(merged.astype(BF16), wout_ref[...], preferred_element_type=F32)
    o_ref[0] = x + mod[:, 2 * d:] * _rms(out, gpost_ref[...], d)


def _merge_call(x, mod, g_pre, g_post, wg, wm, u, wpool, s_pool, y_mla, y_diff, y_swa, w_br, w_out, tile):
    b, n, d = x.shape
    r = tile // POOL_HALO
    last = n // POOL_HALO - 1
    row = lambda w: pl.BlockSpec((1, tile, w), lambda i, bb: (bb, i, 0))
    return pl.pallas_call(
        functools.partial(_merge_kernel, n_total=n),
        grid=(n // tile, b),
        in_specs=[row(d),
                  pl.BlockSpec((1, 1, 3 * d), lambda i, bb: (bb, 0, 0)),
                  _const_spec(g_pre.shape), _const_spec(g_post.shape), _const_spec(wg.shape), _const_spec(wm.shape),
                  pl.BlockSpec((1, POOL_HALO, BRANCH_W), lambda i, bb: (bb, jnp.maximum(i * r - 1, 0), 0)),
                  row(BRANCH_W),
                  pl.BlockSpec((1, POOL_HALO, BRANCH_W), lambda i, bb: (bb, jnp.minimum((i + 1) * r, last), 0)),
                  _const_spec(wpool.shape), _const_spec(s_pool.shape),
                  row(BRANCH_W), row(BRANCH_W), row(BRANCH_W),
                  _const_spec(w_br.shape), _const_spec(w_out.shape)],
        out_specs=row(d),
        out_shape=jax.ShapeDtypeStruct((b, n, d), F32),
        compiler_params=_cparams(("arbitrary", "arbitrary")),
        name="merge",
    )(x, mod, g_pre, g_post, wg, wm, u, u, u, wpool, s_pool, y_mla, y_diff, y_swa, w_br, w_out)


def _rope_tables(n_tok, identity):
    if identity:
        one, zero = jnp.ones((n_tok, LANE), F32), jnp.zeros((n_tok, LANE), F32)
        return jnp.concatenate([one, zero, one, zero, one, zero], axis=1)
    t = jnp.arange(n_tok, dtype=jnp.int32)
    row = (t // GRID_W).astype(F32)[:, None]
    col = (t % GRID_W).astype(F32)[:, None]

    def cs(rot_dim):
        n_freq = rot_dim // 4
        inv = jnp.exp(-math.log(ROPE_BASE) * jnp.arange(n_freq, dtype=F32) / n_freq)
        ang = jnp.concatenate([row * inv, col * inv], axis=-1)
        c, s = jnp.cos(ang), jnp.sin(ang)
        return jnp.concatenate([c, c], axis=1), jnp.concatenate([-s, s], axis=1)

    c32, s32 = cs(DIFF_QK)
    c64, s64 = cs(SWA_HEAD)
    one, zero = jnp.ones((n_tok, MLA_NOPE), F32), jnp.zeros((n_tok, MLA_NOPE), F32)
    cm = jnp.concatenate([one, c32, one[:, :MLA_ROPE]], axis=1)
    sm = jnp.concatenate([zero, s32, zero[:, :MLA_ROPE]], axis=1)
    return jnp.concatenate([jnp.tile(c32, (1, 4)), jnp.tile(s32, (1, 4)), jnp.tile(c64, (1, 2)),
                            jnp.tile(s64, (1, 2)), cm, sm], axis=1)


def _pack_layer(w_in, w_pool, g_cq, w_uq, w_uk, w_br, w_out):
    d = w_in.shape[0]
    col = lambda name: w_in[:, _OFF[name][0]:_OFF[name][1]]
    z = lambda w: jnp.zeros((d, w), w_in.dtype)
    swa_q = col('swa_q')
    swa_blocks = []
    for j in range(SWA_HEADS):
        qj = swa_q[:, SWA_HEAD * j:SWA_HEAD * (j + 1)]
        swa_blocks += [qj, z(SWA_HEAD)] if j // 2 == 0 else [z(SWA_HEAD), qj]
    wa = jnp.concatenate(
        [col('pool_in'), col('mla_cq'), z(256 - MLA_Q_RANK), col('mla_ckv'),
         z(MLA_NOPE), col('mla_kr'), z(LANE - MLA_NOPE - MLA_ROPE),
         col('diff_q'), col('diff_k'), col('diff_v')] + swa_blocks + [col('swa_k'), col('swa_v')], axis=1)
    hq = MLA_NOPE + MLA_ROPE
    uq = jnp.pad(w_uq.reshape(MLA_Q_RANK, MLA_HEADS, hq), ((0, 256 - MLA_Q_RANK), (0, 0), (0, LANE - hq)))
    uk = jnp.pad(w_uk.reshape(MLA_KV_RANK, MLA_HEADS, MLA_NOPE), ((0, 0), (0, 0), (0, LANE - MLA_NOPE)))
    wpool = jnp.zeros((BRANCH_W, BRANCH_W), w_pool.dtype)
    for g in range(len(POOL_WINDOWS)):
        sl = slice(POOL_GROUP * g, POOL_GROUP * (g + 1))
        wpool = wpool.at[sl, sl].set(w_pool[g])
    return dict(
        wa=wa.astype(BF16), wg=col('gates').astype(BF16), wm=col('merge').astype(BF16),
        g_cq=jnp.pad(g_cq, (0, 256 - MLA_Q_RANK)).reshape(1, 256),
        w_uq=uq.reshape(256, MLA_HEADS * LANE).astype(BF16), w_uk=uk.reshape(MLA_KV_RANK, MLA_HEADS * LANE).astype(BF16),
        wpool=wpool.astype(BF16), w_br=w_br.astype(BF16), w_out=w_out.astype(BF16))


def kernel(x, c, ctx, c_ctx, w_mod, b_mod, g_pre, g_post, w_in, w_pool, s_pool, g_cq, w_uq, g_ckv, w_uk, w_uv,
           lam_q1, lam_k1, lam_q2, lam_k2, g_diff, sink, w_br, w_out):
    b, n, d = x.shape
    n_ctx = ctx.shape[1]
    depth = w_in.shape[0]
    tile = min(SEQ_TILE, n)
    assert n % tile == 0 and n_ctx % LANE == 0 and n_ctx <= SEQ_TILE and d % LANE == 0

    rows = -(-(b + 1) // 8) * 8
    cc = jnp.zeros((rows, d), F32).at[:b].set(c).at[b].set(c_ctx)
    mod_all = _mod_call(cc, w_mod, b_mod)
    tabs = _rope_tables(n, False)
    tabs_ctx = _rope_tables(n_ctx, True)

    xc = ctx
    for l in range(depth):
        need_ctx = l < depth - 1
        lam_init = 0.8 - 0.6 * math.exp(-0.3 * l)
        p = _pack_layer(w_in[l], w_pool[l], g_cq[l], w_uq[l], w_uk[l], w_br[l], w_out[l])
        mod = mod_all[l, :b].reshape(b, 1, 3 * d)
        mod_c = jnp.broadcast_to(mod_all[l, b].reshape(1, 1, 3 * d), (b, 1, 3 * d))
        gp, gpo = g_pre[l].reshape(1, d), g_post[l].reshape(1, d)
        gckv = g_ckv[l].reshape(1, MLA_KV_RANK)
        wuv = w_uv[l].astype(BF16)
        lam_v = jnp.stack([lam_q1[l], lam_k1[l], lam_q2[l], lam_k2[l]])
        gd = g_diff[l].reshape(DIFF_V, 1)
        sk = sink[l].reshape(1, SWA_HEADS)
        sp = s_pool[l].reshape(1, BRANCH_W)

        proj = lambda xx, mm, tt, tl: _proj_call(xx, mm, gp, p['wa'], tt, p['g_cq'], p['w_uq'], gckv, p['w_uk'],
                                                 wuv, tl)
        u, mq, mk, mvt, dq, dk, dvt, sq, skk, sv = proj(x, mod, tabs, tile)
        uc, mqc, mkc, mvtc, dqc, dkc, dvtc, sqc, skc, svc = proj(xc, mod_c, tabs_ctx, n_ctx)

        y_mla = _dense_call("mla", mq, mk, mvt, mkc, mvtc, (), lam_init)
        y_diff = _dense_call("diff", dq, dk, dvt, dkc, dvtc, (lam_v, gd), lam_init)
        y_swa = _swa_call(sq, skk, sv, skc, svc, sk, min(SWA_TILE, n))
        merge = lambda xx, mm, uu, ya, yb, yc, tl: _merge_call(xx, mm, gp, gpo, p['wg'], p['wm'], uu, p['wpool'], sp,
                                                              ya, yb, yc, p['w_br'], p['w_out'], tl)
        x_new = merge(x, mod, u, y_mla, y_diff, y_swa, tile)
        if need_ctx:
            yc_mla = _dense_call("mla", mqc, None, None, mkc, mvtc, (), lam_init)
            yc_diff = _dense_call("diff", dqc, None, None, dkc, dvtc, (lam_v, gd), lam_init)
            yc_swa = _swa_call(sqc, None, None, skc, svc, sk, n_ctx)
            xc = merge(xc, mod_c, uc, yc_mla, yc_diff, yc_swa, n_ctx)
        x = x_new
    return x
```

```python
import functools
import math

import jax
import jax.numpy as jnp
from jax import lax
from jax.experimental import pallas as pl
from jax.experimental.pallas import tpu as pltpu

F32 = jnp.float32
BF16 = jnp.bfloat16

GRID_W = 64
EPS = 1e-6
ROPE_BASE = 10000.0
NEG = -1e30
N_BRANCH = 4
BRANCH_W = 256
POOL_WINDOWS = (2, 4, 8, 16)
POOL_GROUP = BRANCH_W // len(POOL_WINDOWS)
POOL_HALO = 8
MLA_HEADS, MLA_NOPE, MLA_ROPE, MLA_V = 4, 64, 32, 64
MLA_Q_RANK, MLA_KV_RANK = 192, 128
DIFF_HEADS, DIFF_QK, DIFF_V = 4, 32, 64
SWA_HEADS, SWA_KV_HEADS, SWA_HEAD, SWA_WINDOW = 4, 2, 64, 128
HEAD_V = 64
LOG2E = math.log2(math.e)

LANE = 128
ONES_ROWS = 16
VMEM_LIMIT = 56 * 1024 * 1024
SEQ_TILE = 512
SWA_TILE = 512
EXP2_GUARD = 60.0
KEY_CHUNK = {"mla": 256, "diff": 256}
Q_TILE = {"mla": 512, "diff": 256}

_SPLITS = (('pool_in', 256), ('mla_cq', 192), ('mla_ckv', 128), ('mla_kr', 32), ('diff_q', 256), ('diff_k', 256),
           ('diff_v', 256), ('swa_q', 256), ('swa_k', 128), ('swa_v', 128), ('gates', 1024), ('merge', 4096))
_OFF = {}
_o = 0
for _n, _w in _SPLITS:
    _OFF[_n] = (_o, _o + _w)
    _o += _w

A_POOL, A_CQ, A_CKV, A_KR, A_DQ, A_DK, A_DV, A_SQ, A_SK, A_SV, A_END = (
    0, 256, 512, 640, 768, 1024, 1280, 1536, 2048, 2176, 2304)


def _cparams(sem):
    return pltpu.CompilerParams(dimension_semantics=sem, vmem_limit_bytes=VMEM_LIMIT)


def _const_spec(shape):
    nd = len(shape)
    return pl.BlockSpec(shape, lambda *_: (0,) * nd, pipeline_mode=pl.Buffered(1))


def _sigmoid(x):
    return 0.5 * jnp.tanh(0.5 * x) + 0.5


def _rms(x, g, n):
    return x * lax.rsqrt(jnp.sum(x * x, axis=-1, keepdims=True) * (1.0 / n) + EPS) * g


def _prenorm(x, mod, g_pre):
    d = x.shape[-1]
    shift, scale = mod[:, :d], mod[:, d:2 * d]
    return _rms(x, g_pre, d) * (1.0 + scale) + shift


def _rope128(x, c, s, period, half, first_below):
    lane = lax.broadcasted_iota(jnp.int32, x.shape, 1)
    first = (lane % period) < first_below
    swapped = jnp.where(first, pltpu.roll(x, LANE - half, 1), pltpu.roll(x, half, 1))
    return x * c + swapped * s


def _rope_mla(x, c, s):
    return _rope128(x, c, s, LANE, MLA_ROPE // 2, MLA_NOPE + MLA_ROPE // 2)


def _rope_heads(x, c, s, head_dim):
    return _rope128(x, c, s, head_dim, head_dim // 2, head_dim // 2)


def _mod_kernel(c_ref, w_ref, b_ref, o_ref):
    a = c_ref[...]
    a = a * jax.nn.sigmoid(a)
    w = w_ref[0]
    a_hi = a.astype(BF16)
    a_lo = (a - a_hi.astype(F32)).astype(BF16)
    w_hi = w.astype(BF16)
    w_lo = (w - w_hi.astype(F32)).astype(BF16)
    acc = jnp.dot(a_hi, w_hi, preferred_element_type=F32)
    acc += jnp.dot(a_lo, w_hi, preferred_element_type=F32)
    acc += jnp.dot(a_hi, w_lo, preferred_element_type=F32)
    o_ref[0] = acc + b_ref[0]


def _mod_call(cc, w_mod, b_mod):
    n_layer, d, d3 = w_mod.shape
    rows = cc.shape[0]
    return pl.pallas_call(
        _mod_kernel,
        grid=(n_layer, d3 // d),
        in_specs=[pl.BlockSpec((rows, d), lambda l, j: (0, 0)),
                  pl.BlockSpec((1, d, d), lambda l, j: (l, 0, j)),
                  pl.BlockSpec((1, 1, d), lambda l, j: (l, 0, j))],
        out_specs=pl.BlockSpec((1, rows, d), lambda l, j: (l, 0, j)),
        out_shape=jax.ShapeDtypeStruct((n_layer, rows, d3), F32),
        compiler_params=_cparams(("arbitrary", "arbitrary")),
        name="mod",
    )(cc, w_mod, b_mod.reshape(n_layer, 1, d3))


def _proj_kernel(x_ref, mod_ref, gpre_ref, wa_ref, tabs_ref, gcq_ref, wuq_ref, gckv_ref, wuk_ref, wuv_ref,
                 u_ref, mq_ref, mk_ref, mvt_ref, dq_ref, dk_ref, dvt_ref, sq_ref, sk_ref, sv_ref):
    hb = _prenorm(x_ref[0], mod_ref[0], gpre_ref[...]).astype(BF16)

    def sect(lo, hi):
        return jnp.dot(hb, wa_ref[:, lo:hi], preferred_element_type=F32)

    tabs = tabs_ref[...]
    c32, s32, c64, s64, cm, sm = (tabs[:, LANE * j:LANE * (j + 1)] for j in range(6))

    u_ref[0] = sect(A_POOL, A_CQ)

    cqn = _rms(sect(A_CQ, A_CKV), gcq_ref[...], MLA_Q_RANK).astype(BF16)
    q = jnp.dot(cqn, wuq_ref[...], preferred_element_type=F32)
    mla_scale = (MLA_NOPE + MLA_ROPE) ** -0.5 * LOG2E
    for h in range(MLA_HEADS):
        blk = _rope_mla(q[:, LANE * h:LANE * (h + 1)], cm, sm)
        mq_ref[0, :, LANE * h:LANE * (h + 1)] = (blk * mla_scale).astype(BF16)
    ckvn = _rms(sect(A_CKV, A_KR), gckv_ref[...], MLA_KV_RANK).astype(BF16)
    k_nope = jnp.dot(ckvn, wuk_ref[...], preferred_element_type=F32)
    k_rope = sect(A_KR, A_DQ)
    for h in range(MLA_HEADS):
        blk = _rope_mla(k_nope[:, LANE * h:LANE * (h + 1)] + k_rope, cm, sm)
        mk_ref[0, h] = blk.astype(BF16)
    mvt_ref[0, 0] = jnp.dot(ckvn, wuv_ref[...], preferred_element_type=F32).T.astype(BF16)

    dq = sect(A_DQ, A_DK)
    dk = sect(A_DK, A_DV)
    diff_scale = DIFF_QK ** -0.5 * LOG2E
    for j in range(2):
        sl = slice(LANE * j, LANE * (j + 1))
        dq_ref[0, :, sl] = (_rope_heads(dq[:, sl], c32, s32, DIFF_QK) * diff_scale).astype(BF16)
        dk_ref[0, :, sl] = _rope_heads(dk[:, sl], c32, s32, DIFF_QK).astype(BF16)
    dvt_ref[0, 0] = sect(A_DV, A_SQ).T.astype(BF16)

    sq = sect(A_SQ, A_SK)
    swa_scale = SWA_HEAD ** -0.5
    for j in range(SWA_HEADS):
        sl = slice(LANE * j, LANE * (j + 1))
        sq_ref[0, :, sl] = (_rope_heads(sq[:, sl], c64, s64, SWA_HEAD) * swa_scale).astype(BF16)
    sk_ref[0] = _rope_heads(sect(A_SK, A_SV), c64, s64, SWA_HEAD).astype(BF16)
    sv_ref[0] = sect(A_SV, A_END).astype(BF16)


def _proj_call(x, mod, g_pre, wa, tabs, g_cq, w_uq, g_ckv, w_uk, w_uv, tile):
    b, n, d = x.shape
    nt = n // tile
    row = lambda w: pl.BlockSpec((1, tile, w), lambda i, bb: (bb, i, 0))
    vt = pl.BlockSpec((1, 1, 2 * LANE, tile), lambda i, bb: (bb, i, 0, 0))
    sds = jax.ShapeDtypeStruct
    return pl.pallas_call(
        _proj_kernel,
        grid=(nt, b),
        in_specs=[row(d),
                  pl.BlockSpec((1, 1, 3 * d), lambda i, bb: (bb, 0, 0)),
                  _const_spec(g_pre.shape), _const_spec(wa.shape),
                  pl.BlockSpec((tile, 6 * LANE), lambda i, bb: (i, 0)),
                  _const_spec(g_cq.shape), _const_spec(w_uq.shape), _const_spec(g_ckv.shape),
                  _const_spec(w_uk.shape), _const_spec(w_uv.shape)],
        out_specs=[row(256), row(512), pl.BlockSpec((1, MLA_HEADS, tile, LANE), lambda i, bb: (bb, 0, i, 0)), vt,
                   row(256), row(256), vt, row(512), row(128), row(128)],
        out_shape=[sds((b, n, 256), F32), sds((b, n, 512), BF16), sds((b, MLA_HEADS, n, LANE), BF16),
                   sds((b, nt, 256, tile), BF16), sds((b, n, 256), BF16), sds((b, n, 256), BF16),
                   sds((b, nt, 256, tile), BF16), sds((b, n, 512), BF16), sds((b, n, 128), BF16),
                   sds((b, n, 128), BF16)],
        compiler_params=_cparams(("arbitrary", "arbitrary")),
        name="proj",
    )(x, mod, g_pre, wa, tabs, g_cq, w_uq, g_ckv, w_uk, w_uv)


def _dense_kernel(*refs, kind, has_lat, lam_init, n_chunks, tk):
    refs = list(refs)
    q_ref = refs.pop(0)
    if has_lat:
        klat_ref, vlat_ref = refs.pop(0), refs.pop(0)
    kctx_ref, vctx_ref = refs.pop(0), refs.pop(0)
    if kind == "diff":
        lam_ref, gd_ref = refs.pop(0), refs.pop(0)
    o_ref = refs.pop(0)
    tq = q_ref.shape[1]
    if has_lat:
        vt_len = vlat_ref.shape[3]

    chains = []
    for j in range(2):
        if kind == "mla":
            chains.append((j, q_ref[0, :, LANE * j:LANE * (j + 1)]))
        else:
            q_all = q_ref[0]
            lane = lax.broadcasted_iota(jnp.int32, q_all.shape, 1)
            for m in range(2):
                lo = 2 * DIFF_QK * j + DIFF_QK * m
                chains.append((j, jnp.where((lane >= lo) & (lane < lo + DIFF_QK), q_all, jnp.zeros_like(q_all))))

    def keys(ref, j, rws):
        return ref[0, j, rws, :] if kind == "mla" else ref[0, rws, :]

    def score(kcs):
        ss = [lax.dot_general(kcs[j], qm, (((1,), (1,)), ((), ())), preferred_element_type=F32) for j, qm in chains]
        return ss, [jnp.max(s, axis=0, keepdims=True) for s in ss]

    def absorb(state, ss, maxes, vtcs):
        new = []
        for (m_run, acc), (j, _), s, mx in zip(state, chains, ss, maxes):
            m_new = jnp.maximum(m_run, mx)
            p = jnp.exp2(s - m_new).astype(BF16)
            new.append((m_new, jnp.exp2(m_run - m_new) * acc + jnp.dot(vtcs[j], p, preferred_element_type=F32)))
        return tuple(new)

    def ext(vt):
        return jnp.concatenate([vt, jnp.ones((ONES_ROWS, vt.shape[1]), BF16)], axis=0)

    def lat_keys(c):
        return [keys(klat_ref, j, slice(c * tk, (c + 1) * tk)) for j in range(2)]

    def lat_vals(c):
        pieces, k = [], c * tk
        while k < (c + 1) * tk:
            pi, off = divmod(k, vt_len)
            n = min((c + 1) * tk - k, vt_len - off)
            pieces.append((k - c * tk, n, pi, off))
            k += n
        return [[(rel, ext(vlat_ref[0, pi, HEAD_V * j:HEAD_V * (j + 1), off:off + n])) for rel, n, pi, off in pieces]
                for j in range(2)]

    def weigh(accs, alphas, ps, c):
        vts = lat_vals(c)
        out = []
        for acc, a, p, (j, _) in zip(accs, alphas, ps, chains):
            acc = acc if a is None else a * acc
            for rel, vt in vts[j]:
                acc = acc + jnp.dot(vt, p[rel:rel + vt.shape[1], :], preferred_element_type=F32)
            out.append(acc)
        return tuple(out)

    def context_state():
        state = tuple((jnp.full((1, tq), NEG, F32), jnp.zeros((HEAD_V + ONES_ROWS, tq), F32)) for _ in chains)
        ss, maxes = score([keys(kctx_ref, j, slice(None)) for j in range(2)])
        return absorb(state, ss, maxes, [ext(vctx_ref[0, 0, HEAD_V * j:HEAD_V * (j + 1), :]) for j in range(2)])

    def exact_path():
        def soften(ms, ss, maxes):
            new_ms, alphas, ps = [], [], []
            for m_run, s, mx in zip(ms, ss, maxes):
                m_new = jnp.maximum(m_run, mx)
                ps.append(jnp.exp2(s - m_new).astype(BF16))
                new_ms.append(m_new)
                alphas.append(jnp.exp2(m_run - m_new))
            return tuple(new_ms), alphas, ps

        scored = {c: score(lat_keys(c)) for c in range(2)}
        state = context_state()
        ms, accs = tuple(m for m, _ in state), tuple(a for _, a in state)
        ms, al_prev, p_prev = soften(ms, *scored.pop(0))
        for c in range(1, n_chunks):
            ms, al_cur, p_cur = soften(ms, *scored.pop(c))
            if c + 1 < n_chunks:
                scored[c + 1] = score(lat_keys(c + 1))
            accs = weigh(accs, al_prev, p_prev, c - 1)
            al_prev, p_prev = al_cur, p_cur
        return weigh(accs, al_prev, p_prev, n_chunks - 1)

    def lagged_path():
        def soften(refs_, ss):
            ps, new_refs, alphas, over = [], [], [], []
            for r, s in zip(refs_, ss):
                mx = jnp.max(s, axis=0, keepdims=True)
                ps.append(jnp.exp2(s - r).astype(BF16))
                r_new = jnp.maximum(r, mx)
                new_refs.append(r_new)
                alphas.append(jnp.exp2(r - r_new))
                over.append(mx - r)
            return ps, new_refs, alphas, over

        def dots(c):
            kcs = lat_keys(c)
            return [lax.dot_general(kcs[j], qm, (((1,), (1,)), ((), ())), preferred_element_type=F32)
                    for j, qm in chains]

        ss = dots(0)
        state = context_state()
        refs_, accs = [m for m, _ in state], tuple(a for _, a in state)
        p_cur, refs_, al_next, excess = soften(refs_, ss)
        al_cur = [None] * len(chains)
        for c in range(n_chunks):
            if c + 1 < n_chunks:
                p_nxt, refs_, al_after, over = soften(refs_, dots(c + 1))
                excess = [jnp.maximum(e, o) for e, o in zip(excess, over)]
            accs = weigh(accs, al_cur, p_cur, c)
            if c + 1 < n_chunks:
                p_cur, al_cur, al_next = p_nxt, al_next, al_after
        worst = excess[0]
        for e in excess[1:]:
            worst = jnp.maximum(worst, e)
        return accs, jnp.max(worst)

    def finish(accs):
        normed = [acc[:HEAD_V] / acc[HEAD_V:HEAD_V + 1] for acc in accs]
        if kind == "mla":
            outs = normed
        else:
            lam_v = lam_ref[...]
            lam = (jnp.exp(jnp.sum(lam_v[0:1] * lam_v[1:2], axis=1, keepdims=True))
                   - jnp.exp(jnp.sum(lam_v[2:3] * lam_v[3:4], axis=1, keepdims=True)) + lam_init)
            outs = []
            for j in range(2):
                o = normed[2 * j] - lam * normed[2 * j + 1]
                o = o * lax.rsqrt(jnp.sum(o * o, axis=0, keepdims=True) * (1.0 / DIFF_V) + EPS)
                outs.append(o * gd_ref[...] * (1.0 - lam_init))
        o_ref[0] = jnp.concatenate(outs, axis=0).T.astype(BF16)

    if not has_lat:
        finish([acc for _, acc in context_state()])
    else:
        accs, worst = lagged_path()
        finish(accs)

        @pl.when(worst > EXP2_GUARD)
        def _():
            finish(exact_path())


def _dense_call(kind, q, k_lat, vt_lat, k_ctx, vt_ctx, extra, lam_init):
    b, n, _ = q.shape
    tq = min(Q_TILE[kind], n)
    assert n % tq == 0
    w = 2 * LANE if kind == "mla" else LANE
    has_lat = k_lat is not None
    c = vt_ctx.shape[3]

    def key_spec(rows):
        if kind == "mla":
            return pl.BlockSpec((1, 2, rows, LANE), lambda bb, hp, i: (bb, hp, 0, 0))
        return pl.BlockSpec((1, rows, LANE), lambda bb, hp, i: (bb, 0, hp))

    args, specs = [q], [pl.BlockSpec((1, tq, w), lambda bb, hp, i: (bb, i, hp))]
    n_chunks = tk = 0
    if has_lat:
        n_lat, vt_len = vt_lat.shape[1] * vt_lat.shape[3], vt_lat.shape[3]
        tk = KEY_CHUNK[kind]
        assert n_lat % tk == 0 and n_lat // tk >= 2
        n_chunks = n_lat // tk
        args += [k_lat, vt_lat]
        specs += [key_spec(n_lat), pl.BlockSpec((1, n_lat // vt_len, LANE, vt_len), lambda bb, hp, i: (bb, 0, hp, 0))]
    args += [k_ctx, vt_ctx]
    specs += [key_spec(c), pl.BlockSpec((1, 1, LANE, c), lambda bb, hp, i: (bb, 0, hp, 0))]
    for e in extra:
        args.append(e)
        specs.append(pl.BlockSpec(e.shape, lambda bb, hp, i: (0, 0)))
    return pl.pallas_call(
        functools.partial(_dense_kernel, kind=kind, has_lat=has_lat, lam_init=lam_init, n_chunks=n_chunks, tk=tk),
        grid=(b, 2, n // tq),
        in_specs=specs,
        out_specs=pl.BlockSpec((1, tq, LANE), lambda bb, hp, i: (bb, i, hp)),
        out_shape=jax.ShapeDtypeStruct((b, n, 2 * LANE), BF16),
        compiler_params=_cparams(("arbitrary", "arbitrary", "arbitrary")),
        name="dense_" + kind + ("_lat" if has_lat else "_ctx"),
    )(*args)


def _swa_kernel(*refs, has_win, n_total):
    refs = list(refs)
    q_ref = refs.pop(0)
    if has_win:
        kp_ref, km_ref, kn_ref, vp_ref, vm_ref, vn_ref = (refs.pop(0) for _ in range(6))
    kc_ref, vc_ref, sink_ref, o_ref = refs
    tq = q_ref.shape[1]
    nt = (((1,), (1,)), ((), ()))
    kc, vc = kc_ref[0], vc_ref[0]
    if has_win:
        t0 = pl.program_id(1) * tq
        kw = jnp.concatenate([kp_ref[0], km_ref[0], kn_ref[0]], axis=0)
        vw = jnp.concatenate([vp_ref[0], vm_ref[0], vn_ref[0]], axis=0)
        nk = kw.shape[0]
        qpos = t0 + lax.broadcasted_iota(jnp.int32, (tq, nk), 0)
        kpos = t0 - SWA_WINDOW + lax.broadcasted_iota(jnp.int32, (tq, nk), 1)
        dist = qpos - kpos
        visible = (dist <= SWA_WINDOW) & (dist >= -SWA_WINDOW) & (kpos >= 0) & (kpos < n_total)
    sink = sink_ref[...]
    res = []
    for j in range(SWA_HEADS):
        qj = q_ref[0, :, LANE * j:LANE * (j + 1)]
        sink_j = sink[:, j:j + 1]
        s_c = lax.dot_general(qj, kc, nt, preferred_element_type=F32)
        m = jnp.maximum(jnp.max(s_c, axis=1, keepdims=True), sink_j)
        if has_win:
            s_w = jnp.where(visible, lax.dot_general(qj, kw, nt, preferred_element_type=F32), NEG)
            m = jnp.maximum(m, jnp.max(s_w, axis=1, keepdims=True))
        p_c = jnp.exp(s_c - m)
        den = jnp.sum(p_c, axis=1, keepdims=True) + jnp.exp(sink_j - m)
        num = jnp.dot(p_c.astype(BF16), vc, preferred_element_type=F32)
        if has_win:
            p_w = jnp.exp(s_w - m)
            den = den + jnp.sum(p_w, axis=1, keepdims=True)
            num = num + jnp.dot(p_w.astype(BF16), vw, preferred_element_type=F32)
        res.append(num / den)
    lane = lax.broadcasted_iota(jnp.int32, (tq, LANE), 1)
    low = lane < SWA_HEAD
    o_ref[0, :, 0:LANE] = jnp.where(low, res[0], pltpu.roll(res[1], SWA_HEAD, 1)).astype(BF16)
    o_ref[0, :, LANE:2 * LANE] = jnp.where(low, pltpu.roll(res[2], SWA_HEAD, 1), res[3]).astype(BF16)


def _swa_call(q, k, v, k_ctx, v_ctx, sink, tile):
    b, n, _ = q.shape
    has_win = k is not None
    c = k_ctx.shape[1]
    args, specs = [q], [pl.BlockSpec((1, tile, 4 * LANE), lambda bb, i: (bb, i, 0))]
    if has_win:
        r = tile // SWA_WINDOW
        last = n // SWA_WINDOW - 1
        prev = pl.BlockSpec((1, SWA_WINDOW, LANE), lambda bb, i: (bb, jnp.maximum(i * r - 1, 0), 0))
        main = pl.BlockSpec((1, tile, LANE), lambda bb, i: (bb, i, 0))
        nxt = pl.BlockSpec((1, SWA_WINDOW, LANE), lambda bb, i: (bb, jnp.minimum((i + 1) * r, last), 0))
        args += [k, k, k, v, v, v]
        specs += [prev, main, nxt, prev, main, nxt]
    args += [k_ctx, v_ctx, sink]
    specs += [pl.BlockSpec((1, c, LANE), lambda bb, i: (bb, 0, 0)),
              pl.BlockSpec((1, c, LANE), lambda bb, i: (bb, 0, 0)),
              pl.BlockSpec(sink.shape, lambda bb, i: (0, 0))]
    return pl.pallas_call(
        functools.partial(_swa_kernel, has_win=has_win, n_total=n),
        grid=(b, n // tile),
        in_specs=specs,
        out_specs=pl.BlockSpec((1, tile, 2 * LANE), lambda bb, i: (bb, i, 0)),
        out_shape=jax.ShapeDtypeStruct((b, n, 2 * LANE), BF16),
        compiler_params=_cparams(("arbitrary", "arbitrary")),
        name="swa_win" if has_win else "swa_ctx",
    )(*args)


def _pool(u, u_prev, u_next, t0, n_total):
    tile = u.shape[0]
    ext = jnp.concatenate([jnp.where(t0 > 0, u_prev, 0.0), u, jnp.where(t0 + tile < n_total, u_next, 0.0)], axis=0)
    n_ext = ext.shape[0]

    def ahead(a, k):
        return pltpu.roll(a, n_ext - k, 0) if k else a

    sums, run, w = [], ext, 1
    for win in POOL_WINDOWS:
        while w < win:
            run = run + ahead(run, w)
            w *= 2
        sums.append(ahead(run, POOL_HALO - win // 2)[:tile])
    grp = lax.broadcasted_iota(jnp.int32, (tile, BRANCH_W), 1) // POOL_GROUP
    t = t0 + lax.broadcasted_iota(jnp.int32, (tile, BRANCH_W), 0)
    win_sum, half = sums[-1], jnp.full((tile, BRANCH_W), POOL_WINDOWS[-1] // 2, jnp.int32)
    for g in range(len(POOL_WINDOWS) - 2, -1, -1):
        win_sum = jnp.where(grp == g, sums[g], win_sum)
        half = jnp.where(grp == g, POOL_WINDOWS[g] // 2, half)
    lo = jnp.clip(t - half, 0, n_total)
    hi = jnp.clip(t + half, 0, n_total)
    return win_sum / (hi - lo).astype(F32) - u


def _merge_kernel(x_ref, mod_ref, gpre_ref, gpost_ref, wg_ref, wm_ref, up_ref, u_ref, un_ref, wpool_ref, spool_ref,
                  ymla_ref, ydiff_ref, yswa_ref, wbr_ref, wout_ref, o_ref, *, n_total):
    x = x_ref[0]
    d = x.shape[-1]
    tile = x.shape[0]
    mod = mod_ref[0]
    hb = _prenorm(x, mod, gpre_ref[...]).astype(BF16)

    pooled = _pool(u_ref[0], up_ref[0], un_ref[0], pl.program_id(0) * tile, n_total)
    y_pool = jnp.dot(pooled.astype(BF16), wpool_ref[...], preferred_element_type=F32) * spool_ref[...]
    ys = [y_pool, ymla_ref[0].astype(F32), ydiff_ref[0].astype(F32), yswa_ref[0].astype(F32)]

    merged = jnp.zeros((tile, d), F32)
    for r in range(N_BRANCH):
        g = jnp.dot(hb, wg_ref[:, BRANCH_W * r:BRANCH_W * (r + 1)], preferred_element_type=F32)
        g = g * _sigmoid(g)
        mg = _sigmoid(jnp.dot(hb, wm_ref[:, d * r:d * (r + 1)], preferred_element_type=F32))
        merged = merged + mg * jnp.dot((ys[r] * g).astype(BF16), wbr_ref[r], preferred_element_type=F32)
    out = jnp.dot(merged.astype(BF16), wout_ref[...], preferred_element_type=F32)
    o_ref[0] = x + mod[:, 2 * d:] * _rms(out, gpost_ref[...], d)


def _merge_call(x, mod, g_pre, g_post, wg, wm, u, wpool, s_pool, y_mla, y_diff, y_swa, w_br, w_out, tile):
    b, n, d = x.shape
    r = tile // POOL_HALO
    last = n // POOL_HALO - 1
    row = lambda w: pl.BlockSpec((1, tile, w), lambda i, bb: (bb, i, 0))
    return pl.pallas_call(
        functools.partial(_merge_kernel, n_total=n),
        grid=(n // tile, b),
        in_specs=[row(d),
                  pl.BlockSpec((1, 1, 3 * d), lambda i, bb: (bb, 0, 0)),
                  _const_spec(g_pre.shape), _const_spec(g_post.shape), _const_spec(wg.shape), _const_spec(wm.shape),
                  pl.BlockSpec((1, POOL_HALO, BRANCH_W), lambda i, bb: (bb, jnp.maximum(i * r - 1, 0), 0)),
                  row(BRANCH_W),
                  pl.BlockSpec((1, POOL_HALO, BRANCH_W), lambda i, bb: (bb, jnp.minimum((i + 1) * r, last), 0)),
                  _const_spec(wpool.shape), _const_spec(s_pool.shape),
                  row(BRANCH_W), row(BRANCH_W), row(BRANCH_W),
                  _const_spec(w_br.shape), _const_spec(w_out.shape)],
        out_specs=row(d),
        out_shape=jax.ShapeDtypeStruct((b, n, d), F32),
        compiler_params=_cparams(("arbitrary", "arbitrary")),
        name="merge",
    )(x, mod, g_pre, g_post, wg, wm, u, u, u, wpool, s_pool, y_mla, y_diff, y_swa, w_br, w_out)


def _rope_tables(n_tok, identity):
    if identity:
        one, zero = jnp.ones((n_tok, LANE), F32), jnp.zeros((n_tok, LANE), F32)
        return jnp.concatenate([one, zero, one, zero, one, zero], axis=1)
    t = jnp.arange(n_tok, dtype=jnp.int32)
    row = (t // GRID_W).astype(F32)[:, None]
    col = (t % GRID_W).astype(F32)[:, None]

    def cs(rot_dim):
        n_freq = rot_dim // 4
        inv = jnp.exp(-math.log(ROPE_BASE) * jnp.arange(n_freq, dtype=F32) / n_freq)
        ang = jnp.concatenate([row * inv, col * inv], axis=-1)
        c, s = jnp.cos(ang), jnp.sin(ang)
        return jnp.concatenate([c, c], axis=1), jnp.concatenate([-s, s], axis=1)

    c32, s32 = cs(DIFF_QK)
    c64, s64 = cs(SWA_HEAD)
    one, zero = jnp.ones((n_tok, MLA_NOPE), F32), jnp.zeros((n_tok, MLA_NOPE), F32)
    cm = jnp.concatenate([one, c32, one[:, :MLA_ROPE]], axis=1)
    sm = jnp.concatenate([zero, s32, zero[:, :MLA_ROPE]], axis=1)
    return jnp.concatenate([jnp.tile(c32, (1, 4)), jnp.tile(s32, (1, 4)), jnp.tile(c64, (1, 2)),
                            jnp.tile(s64, (1, 2)), cm, sm], axis=1)


def _pack_layer(w_in, w_pool, g_cq, w_uq, w_uk, w_br, w_out):
    d = w_in.shape[0]
    col = lambda name: w_in[:, _OFF[name][0]:_OFF[name][1]]
    z = lambda w: jnp.zeros((d, w), w_in.dtype)
    swa_q = col('swa_q')
    swa_blocks = []
    for j in range(SWA_HEADS):
        qj = swa_q[:, SWA_HEAD * j:SWA_HEAD * (j + 1)]
        swa_blocks += [qj, z(SWA_HEAD)] if j // 2 == 0 else [z(SWA_HEAD), qj]
    wa = jnp.concatenate(
        [col('pool_in'), col('mla_cq'), z(256 - MLA_Q_RANK), col('mla_ckv'),
         z(MLA_NOPE), col('mla_kr'), z(LANE - MLA_NOPE - MLA_ROPE),
         col('diff_q'), col('diff_k'), col('diff_v')] + swa_blocks + [col('swa_k'), col('swa_v')], axis=1)
    hq = MLA_NOPE + MLA_ROPE
    uq = jnp.pad(w_uq.reshape(MLA_Q_RANK, MLA_HEADS, hq), ((0, 256 - MLA_Q_RANK), (0, 0), (0, LANE - hq)))
    uk = jnp.pad(w_uk.reshape(MLA_KV_RANK, MLA_HEADS, MLA_NOPE), ((0, 0), (0, 0), (0, LANE - MLA_NOPE)))
    wpool = jnp.zeros((BRANCH_W, BRANCH_W), w_pool.dtype)
    for g in range(len(POOL_WINDOWS)):
        sl = slice(POOL_GROUP * g, POOL_GROUP * (g + 1))
        wpool = wpool.at[sl, sl].set(w_pool[g])
    return dict(
        wa=wa.astype(BF16), wg=col('gates').astype(BF16), wm=col('merge').astype(BF16),
        g_cq=jnp.pad(g_cq, (0, 256 - MLA_Q_RANK)).reshape(1, 256),
        w_uq=uq.reshape(256, MLA_HEADS * LANE).astype(BF16), w_uk=uk.reshape(MLA_KV_RANK, MLA_HEADS * LANE).astype(BF16),
        wpool=wpool.astype(BF16), w_br=w_br.astype(BF16), w_out=w_out.astype(BF16))


def kernel(x, c, ctx, c_ctx, w_mod, b_mod, g_pre, g_post, w_in, w_pool, s_pool, g_cq, w_uq, g_ckv, w_uk, w_uv,
           lam_q1, lam_k1, lam_q2, lam_k2, g_diff, sink, w_br, w_out):
    b, n, d = x.shape
    n_ctx = ctx.shape[1]
    depth = w_in.shape[0]
    tile = min(SEQ_TILE, n)
    assert n % tile == 0 and n_ctx % LANE == 0 and n_ctx <= SEQ_TILE and d % LANE == 0

    rows = -(-(b + 1) // 8) * 8
    cc = jnp.zeros((rows, d), F32).at[:b].set(c).at[b].set(c_ctx)
    mod_all = _mod_call(cc, w_mod, b_mod)
    tabs = _rope_tables(n, False)
    tabs_ctx = _rope_tables(n_ctx, True)

    xc = ctx
    for l in range(depth):
        need_ctx = l < depth - 1
        lam_init = 0.8 - 0.6 * math.exp(-0.3 * l)
        p = _pack_layer(w_in[l], w_pool[l], g_cq[l], w_uq[l], w_uk[l], w_br[l], w_out[l])
        mod = mod_all[l, :b].reshape(b, 1, 3 * d)
        mod_c = jnp.broadcast_to(mod_all[l, b].reshape(1, 1, 3 * d), (b, 1, 3 * d))
        gp, gpo = g_pre[l].reshape(1, d), g_post[l].reshape(1, d)
        gckv = g_ckv[l].reshape(1, MLA_KV_RANK)
        wuv = w_uv[l].astype(BF16)
        lam_v = jnp.stack([lam_q1[l], lam_k1[l], lam_q2[l], lam_k2[l]])
        gd = g_diff[l].reshape(DIFF_V, 1)
        sk = sink[l].reshape(1, SWA_HEADS)
        sp = s_pool[l].reshape(1, BRANCH_W)

        proj = lambda xx, mm, tt, tl: _proj_call(xx, mm, gp, p['wa'], tt, p['g_cq'], p['w_uq'], gckv, p['w_uk'],
                                                 wuv, tl)
        u, mq, mk, mvt, dq, dk, dvt, sq, skk, sv = proj(x, mod, tabs, tile)
        uc, mqc, mkc, mvtc, dqc, dkc, dvtc, sqc, skc, svc = proj(xc, mod_c, tabs_ctx, n_ctx)

        y_mla = _dense_call("mla", mq, mk, mvt, mkc, mvtc, (), lam_init)
        y_diff = _dense_call("diff", dq, dk, dvt, dkc, dvtc, (lam_v, gd), lam_init)
        y_swa = _swa_call(sq, skk, sv, skc, svc, sk, min(SWA_TILE, n))
        merge = lambda xx, mm, uu, ya, yb, yc, tl: _merge_call(xx, mm, gp, gpo, p['wg'], p['wm'], uu, p['wpool'], sp,
                                                              ya, yb, yc, p['w_br'], p['w_out'], tl)
        x_new = merge(x, mod, u, y_mla, y_diff, y_swa, tile)
        if need_ctx:
            yc_mla = _dense_call("mla", mqc, None, None, mkc, mvtc, (), lam_init)
            yc_diff = _dense_call("diff", dqc, None, None, dkc, dvtc, (lam_v, gd), lam_init)
            yc_swa = _swa_call(sqc, None, None, skc, svc, sk, n_ctx)
            xc = merge(xc, mod_c, uc, yc_mla, yc_diff, yc_swa, n_ctx)
        x = x_new
    return x
```

```python
import functools
import math

import jax
import jax.numpy as jnp
from jax import lax
from jax.experimental import pallas as pl
from jax.experimental.pallas import tpu as pltpu

F32 = jnp.float32
BF16 = jnp.bfloat16

GRID_W = 64
EPS = 1e-6
ROPE_BASE = 10000.0
NEG = -1e30
N_BRANCH = 4
BRANCH_W = 256
POOL_WINDOWS = (2, 4, 8, 16)
POOL_GROUP = BRANCH_W // len(POOL_WINDOWS)
POOL_HALO = 8
MLA_HEADS, MLA_NOPE, MLA_ROPE, MLA_V = 4, 64, 32, 64
MLA_Q_RANK, MLA_KV_RANK = 192, 128
DIFF_HEADS, DIFF_QK, DIFF_V = 4, 32, 64
SWA_HEADS, SWA_KV_HEADS, SWA_HEAD, SWA_WINDOW = 4, 2, 64, 128
HEAD_V = 64
LOG2E = math.log2(math.e)

LANE = 128
ONES_ROWS = 16
VMEM_LIMIT = 56 * 1024 * 1024
SEQ_TILE = 512
SWA_TILE = 512
EXP2_GUARD = 60.0
KEY_CHUNK = {"mla": 256, "diff": 256}
Q_TILE = {"mla": 512, "diff": 256}

_SPLITS = (('pool_in', 256), ('mla_cq', 192), ('mla_ckv', 128), ('mla_kr', 32), ('diff_q', 256), ('diff_k', 256),
           ('diff_v', 256), ('swa_q', 256), ('swa_k', 128), ('swa_v', 128), ('gates', 1024), ('merge', 4096))
_OFF = {}
_o = 0
for _n, _w in _SPLITS:
    _OFF[_n] = (_o, _o + _w)
    _o += _w

A_POOL, A_CQ, A_CKV, A_KR, A_DQ, A_DK, A_DV, A_SQ, A_SK, A_SV, A_END = (
    0, 256, 512, 640, 768, 1024, 1280, 1536, 2048, 2176, 2304)


def _cparams(sem):
    return pltpu.CompilerParams(dimension_semantics=sem, vmem_limit_bytes=VMEM_LIMIT)


def _const_spec(shape):
    nd = len(shape)
    return pl.BlockSpec(shape, lambda *_: (0,) * nd, pipeline_mode=pl.Buffered(1))


def _sigmoid(x):
    return 0.5 * jnp.tanh(0.5 * x) + 0.5


def _rms(x, g, n):
    return x * lax.rsqrt(jnp.sum(x * x, axis=-1, keepdims=True) * (1.0 / n) + EPS) * g


def _prenorm(x, mod, g_pre):
    d = x.shape[-1]
    shift, scale = mod[:, :d], mod[:, d:2 * d]
    return _rms(x, g_pre, d) * (1.0 + scale) + shift


def _rope128(x, c, s, period, half, first_below):
    lane = lax.broadcasted_iota(jnp.int32, x.shape, 1)
    first = (lane % period) < first_below
    swapped = jnp.where(first, pltpu.roll(x, LANE - half, 1), pltpu.roll(x, half, 1))
    return x * c + swapped * s


def _rope_mla(x, c, s):
    return _rope128(x, c, s, LANE, MLA_ROPE // 2, MLA_NOPE + MLA_ROPE // 2)


def _rope_heads(x, c, s, head_dim):
    return _rope128(x, c, s, head_dim, head_dim // 2, head_dim // 2)


def _mod_kernel(c_ref, w_ref, b_ref, o_ref):
    a = c_ref[...]
    a = a * jax.nn.sigmoid(a)
    w = w_ref[0]
    a_hi = a.astype(BF16)
    a_lo = (a - a_hi.astype(F32)).astype(BF16)
    w_hi = w.astype(BF16)
    w_lo = (w - w_hi.astype(F32)).astype(BF16)
    acc = jnp.dot(a_hi, w_hi, preferred_element_type=F32)
    acc += jnp.dot(a_lo, w_hi, preferred_element_type=F32)
    acc += jnp.dot(a_hi, w_lo, preferred_element_type=F32)
    o_ref[0] = acc + b_ref[0]


def _mod_call(cc, w_mod, b_mod):
    n_layer, d, d3 = w_mod.shape
    rows = cc.shape[0]
    return pl.pallas_call(
        _mod_kernel,
        grid=(n_layer, d3 // d),
        in_specs=[pl.BlockSpec((rows, d), lambda l, j: (0, 0)),
                  pl.BlockSpec((1, d, d), lambda l, j: (l, 0, j)),
                  pl.BlockSpec((1, 1, d), lambda l, j: (l, 0, j))],
        out_specs=pl.BlockSpec((1, rows, d), lambda l, j: (l, 0, j)),
        out_shape=jax.ShapeDtypeStruct((n_layer, rows, d3), F32),
        compiler_params=_cparams(("arbitrary", "arbitrary")),
        name="mod",
    )(cc, w_mod, b_mod.reshape(n_layer, 1, d3))


def _proj_kernel(x_ref, mod_ref, gpre_ref, wa_ref, tabs_ref, gcq_ref, wuq_ref, gckv_ref, wuk_ref, wuv_ref,
                 u_ref, mq_ref, mk_ref, mvt_ref, dq_ref, dk_ref, dvt_ref, sq_ref, sk_ref, svt_ref):
    hb = _prenorm(x_ref[0], mod_ref[0], gpre_ref[...]).astype(BF16)

    def sect(lo, hi):
        return jnp.dot(hb, wa_ref[:, lo:hi], preferred_element_type=F32)

    tabs = tabs_ref[...]
    c32, s32, c64, s64, cm, sm = (tabs[:, LANE * j:LANE * (j + 1)] for j in range(6))

    u_ref[0] = sect(A_POOL, A_CQ)

    cqn = _rms(sect(A_CQ, A_CKV), gcq_ref[...], MLA_Q_RANK).astype(BF16)
    q = jnp.dot(cqn, wuq_ref[...], preferred_element_type=F32)
    mla_scale = (MLA_NOPE + MLA_ROPE) ** -0.5 * LOG2E
    for h in range(MLA_HEADS):
        blk = _rope_mla(q[:, LANE * h:LANE * (h + 1)], cm, sm)
        mq_ref[0, :, LANE * h:LANE * (h + 1)] = (blk * mla_scale).astype(BF16)
    ckvn = _rms(sect(A_CKV, A_KR), gckv_ref[...], MLA_KV_RANK).astype(BF16)
    k_nope = jnp.dot(ckvn, wuk_ref[...], preferred_element_type=F32)
    k_rope = sect(A_KR, A_DQ)
    for h in range(MLA_HEADS):
        blk = _rope_mla(k_nope[:, LANE * h:LANE * (h + 1)] + k_rope, cm, sm)
        mk_ref[0, h] = blk.astype(BF16)
    mvt_ref[0, 0] = jnp.dot(ckvn, wuv_ref[...], preferred_element_type=F32).T.astype(BF16)

    dq = sect(A_DQ, A_DK)
    dk = sect(A_DK, A_DV)
    diff_scale = DIFF_QK ** -0.5 * LOG2E
    for j in range(2):
        sl = slice(LANE * j, LANE * (j + 1))
        dq_ref[0, :, sl] = (_rope_heads(dq[:, sl], c32, s32, DIFF_QK) * diff_scale).astype(BF16)
        dk_ref[0, :, sl] = _rope_heads(dk[:, sl], c32, s32, DIFF_QK).astype(BF16)
    dvt_ref[0, 0] = sect(A_DV, A_SQ).T.astype(BF16)

    sq = sect(A_SQ, A_SK)
    swa_scale = SWA_HEAD ** -0.5 * LOG2E
    for j in range(SWA_HEADS):
        sl = slice(LANE * j, LANE * (j + 1))
        sq_ref[0, :, sl] = (_rope_heads(sq[:, sl], c64, s64, SWA_HEAD) * swa_scale).astype(BF16)
    sk_ref[0] = _rope_heads(sect(A_SK, A_SV), c64, s64, SWA_HEAD).astype(BF16)
    svt_ref[0, 0] = sect(A_SV, A_END).T.astype(BF16)


def _proj_call(x, mod, g_pre, wa, tabs, g_cq, w_uq, g_ckv, w_uk, w_uv, tile):
    b, n, d = x.shape
    nt = n // tile
    row = lambda w: pl.BlockSpec((1, tile, w), lambda i, bb: (bb, i, 0))
    vt = pl.BlockSpec((1, 1, 2 * LANE, tile), lambda i, bb: (bb, i, 0, 0))
    sds = jax.ShapeDtypeStruct
    return pl.pallas_call(
        _proj_kernel,
        grid=(nt, b),
        in_specs=[row(d),
                  pl.BlockSpec((1, 1, 3 * d), lambda i, bb: (bb, 0, 0)),
                  _const_spec(g_pre.shape), _const_spec(wa.shape),
                  pl.BlockSpec((tile, 6 * LANE), lambda i, bb: (i, 0)),
                  _const_spec(g_cq.shape), _const_spec(w_uq.shape), _const_spec(g_ckv.shape),
                  _const_spec(w_uk.shape), _const_spec(w_uv.shape)],
        out_specs=[row(256), row(512), pl.BlockSpec((1, MLA_HEADS, tile, LANE), lambda i, bb: (bb, 0, i, 0)), vt,
                   row(256), row(256), vt, row(512), row(128),
                   pl.BlockSpec((1, 1, LANE, tile), lambda i, bb: (bb, i, 0, 0))],
        out_shape=[sds((b, n, 256), F32), sds((b, n, 512), BF16), sds((b, MLA_HEADS, n, LANE), BF16),
                   sds((b, nt, 256, tile), BF16), sds((b, n, 256), BF16), sds((b, n, 256), BF16),
                   sds((b, nt, 256, tile), BF16), sds((b, n, 512), BF16), sds((b, n, 128), BF16),
                   sds((b, nt, LANE, tile), BF16)],
        compiler_params=_cparams(("arbitrary", "arbitrary")),
        name="proj",
    )(x, mod, g_pre, wa, tabs, g_cq, w_uq, g_ckv, w_uk, w_uv)


def _dense_kernel(*refs, kind, has_lat, lam_init, n_chunks, tk):
    refs = list(refs)
    q_ref = refs.pop(0)
    if has_lat:
        klat_ref, vlat_ref = refs.pop(0), refs.pop(0)
    kctx_ref, vctx_ref = refs.pop(0), refs.pop(0)
    if kind == "diff":
        lam_ref, gd_ref = refs.pop(0), refs.pop(0)
    o_ref = refs.pop(0)
    tq = q_ref.shape[1]
    if has_lat:
        vt_len = vlat_ref.shape[3]

    chains = []
    for j in range(2):
        if kind == "mla":
            chains.append((j, q_ref[0, :, LANE * j:LANE * (j + 1)]))
        else:
            q_all = q_ref[0]
            lane = lax.broadcasted_iota(jnp.int32, q_all.shape, 1)
            for m in range(2):
                lo = 2 * DIFF_QK * j + DIFF_QK * m
                chains.append((j, jnp.where((lane >= lo) & (lane < lo + DIFF_QK), q_all, jnp.zeros_like(q_all))))

    def keys(ref, j, rws):
        return ref[0, j, rws, :] if kind == "mla" else ref[0, rws, :]

    def score(kcs):
        ss = [lax.dot_general(kcs[j], qm, (((1,), (1,)), ((), ())), preferred_element_type=F32) for j, qm in chains]
        return ss, [jnp.max(s, axis=0, keepdims=True) for s in ss]

    def absorb(state, ss, maxes, vtcs):
        new = []
        for (m_run, acc), (j, _), s, mx in zip(state, chains, ss, maxes):
            m_new = jnp.maximum(m_run, mx)
            p = jnp.exp2(s - m_new).astype(BF16)
            new.append((m_new, jnp.exp2(m_run - m_new) * acc + jnp.dot(vtcs[j], p, preferred_element_type=F32)))
        return tuple(new)

    def ext(vt):
        return jnp.concatenate([vt, jnp.ones((ONES_ROWS, vt.shape[1]), BF16)], axis=0)

    def lat_keys(c):
        return [keys(klat_ref, j, slice(c * tk, (c + 1) * tk)) for j in range(2)]

    def lat_vals(c):
        pieces, k = [], c * tk
        while k < (c + 1) * tk:
            pi, off = divmod(k, vt_len)
            n = min((c + 1) * tk - k, vt_len - off)
            pieces.append((k - c * tk, n, pi, off))
            k += n
        return [[(rel, ext(vlat_ref[0, pi, HEAD_V * j:HEAD_V * (j + 1), off:off + n])) for rel, n, pi, off in pieces]
                for j in range(2)]

    def weigh(accs, alphas, ps, c):
        vts = lat_vals(c)
        out = []
        for acc, a, p, (j, _) in zip(accs, alphas, ps, chains):
            acc = acc if a is None else a * acc
            for rel, vt in vts[j]:
                acc = acc + jnp.dot(vt, p[rel:rel + vt.shape[1], :], preferred_element_type=F32)
            out.append(acc)
        return tuple(out)

    def context_state():
        state = tuple((jnp.full((1, tq), NEG, F32), jnp.zeros((HEAD_V + ONES_ROWS, tq), F32)) for _ in chains)
        ss, maxes = score([keys(kctx_ref, j, slice(None)) for j in range(2)])
        return absorb(state, ss, maxes, [ext(vctx_ref[0, 0, HEAD_V * j:HEAD_V * (j + 1), :]) for j in range(2)])

    def exact_path():
        def soften(ms, ss, maxes):
            new_ms, alphas, ps = [], [], []
            for m_run, s, mx in zip(ms, ss, maxes):
                m_new = jnp.maximum(m_run, mx)
                ps.append(jnp.exp2(s - m_new).astype(BF16))
                new_ms.append(m_new)
                alphas.append(jnp.exp2(m_run - m_new))
            return tuple(new_ms), alphas, ps

        scored = {c: score(lat_keys(c)) for c in range(2)}
        state = context_state()
        ms, accs = tuple(m for m, _ in state), tuple(a for _, a in state)
        ms, al_prev, p_prev = soften(ms, *scored.pop(0))
        for c in range(1, n_chunks):
            ms, al_cur, p_cur = soften(ms, *scored.pop(c))
            if c + 1 < n_chunks:
                scored[c + 1] = score(lat_keys(c + 1))
            accs = weigh(accs, al_prev, p_prev, c - 1)
            al_prev, p_prev = al_cur, p_cur
        return weigh(accs, al_prev, p_prev, n_chunks - 1)

    def lagged_path():
        def soften(refs_, ss):
            ps, new_refs, alphas, over = [], [], [], []
            for r, s in zip(refs_, ss):
                mx = jnp.max(s, axis=0, keepdims=True)
                ps.append(jnp.exp2(s - r).astype(BF16))
                r_new = jnp.maximum(r, mx)
                new_refs.append(r_new)
                alphas.append(jnp.exp2(r - r_new))
                over.append(mx - r)
            return ps, new_refs, alphas, over

        def dots(c):
            kcs = lat_keys(c)
            return [lax.dot_general(kcs[j], qm, (((1,), (1,)), ((), ())), preferred_element_type=F32)
                    for j, qm in chains]

        ss = dots(0)
        state = context_state()
        refs_, accs = [m for m, _ in state], tuple(a for _, a in state)
        p_cur, refs_, al_next, excess = soften(refs_, ss)
        al_cur = [None] * len(chains)
        for c in range(n_chunks):
            if c + 1 < n_chunks:
                p_nxt, refs_, al_after, over = soften(refs_, dots(c + 1))
                excess = [jnp.maximum(e, o) for e, o in zip(excess, over)]
            accs = weigh(accs, al_cur, p_cur, c)
            if c + 1 < n_chunks:
                p_cur, al_cur, al_next = p_nxt, al_next, al_after
        worst = excess[0]
        for e in excess[1:]:
            worst = jnp.maximum(worst, e)
        return accs, jnp.max(worst)

    def finish(accs):
        normed = [acc[:HEAD_V] / acc[HEAD_V:HEAD_V + 1] for acc in accs]
        if kind == "mla":
            outs = normed
        else:
            lam_v = lam_ref[...]
            lam = (jnp.exp(jnp.sum(lam_v[0:1] * lam_v[1:2], axis=1, keepdims=True))
                   - jnp.exp(jnp.sum(lam_v[2:3] * lam_v[3:4], axis=1, keepdims=True)) + lam_init)
            outs = []
            for j in range(2):
                o = normed[2 * j] - lam * normed[2 * j + 1]
                o = o * lax.rsqrt(jnp.sum(o * o, axis=0, keepdims=True) * (1.0 / DIFF_V) + EPS)
                outs.append(o * gd_ref[...] * (1.0 - lam_init))
        o_ref[0] = jnp.concatenate(outs, axis=0).T.astype(BF16)

    if not has_lat:
        finish([acc for _, acc in context_state()])
    else:
        accs, worst = lagged_path()
        finish(accs)

        @pl.when(worst > EXP2_GUARD)
        def _():
            finish(exact_path())


def _dense_call(kind, q, k_lat, vt_lat, k_ctx, vt_ctx, extra, lam_init):
    b, n, _ = q.shape
    tq = min(Q_TILE[kind], n)
    assert n % tq == 0
    w = 2 * LANE if kind == "mla" else LANE
    has_lat = k_lat is not None
    c = vt_ctx.shape[3]

    def key_spec(rows):
        if kind == "mla":
            return pl.BlockSpec((1, 2, rows, LANE), lambda bb, hp, i: (bb, hp, 0, 0))
        return pl.BlockSpec((1, rows, LANE), lambda bb, hp, i: (bb, 0, hp))

    args, specs = [q], [pl.BlockSpec((1, tq, w), lambda bb, hp, i: (bb, i, hp))]
    n_chunks = tk = 0
    if has_lat:
        n_lat, vt_len = vt_lat.shape[1] * vt_lat.shape[3], vt_lat.shape[3]
        tk = KEY_CHUNK[kind]
        assert n_lat % tk == 0 and n_lat // tk >= 2
        n_chunks = n_lat // tk
        args += [k_lat, vt_lat]
        specs += [key_spec(n_lat), pl.BlockSpec((1, n_lat // vt_len, LANE, vt_len), lambda bb, hp, i: (bb, 0, hp, 0))]
    args += [k_ctx, vt_ctx]
    specs += [key_spec(c), pl.BlockSpec((1, 1, LANE, c), lambda bb, hp, i: (bb, 0, hp, 0))]
    for e in extra:
        args.append(e)
        specs.append(pl.BlockSpec(e.shape, lambda bb, hp, i: (0, 0)))
    return pl.pallas_call(
        functools.partial(_dense_kernel, kind=kind, has_lat=has_lat, lam_init=lam_init, n_chunks=n_chunks, tk=tk),
        grid=(b, 2, n // tq),
        in_specs=specs,
        out_specs=pl.BlockSpec((1, tq, LANE), lambda bb, hp, i: (bb, i, hp)),
        out_shape=jax.ShapeDtypeStruct((b, n, 2 * LANE), BF16),
        compiler_params=_cparams(("arbitrary", "arbitrary", "arbitrary")),
        name="dense_" + kind + ("_lat" if has_lat else "_ctx"),
    )(*args)


def _swa_kernel(*refs, has_win, n_total):
    refs = list(refs)
    q_ref = refs.pop(0)
    if has_win:
        kp_ref, km_ref, kn_ref, vp_ref, vm_ref, vn_ref = (refs.pop(0) for _ in range(6))
    kc_ref, vc_ref, sink_ref, o_ref = refs
    tq = q_ref.shape[1]
    nt = (((1,), (1,)), ((), ()))
    kc, vct = kc_ref[0], vc_ref[0, 0]
    if has_win:
        t0 = pl.program_id(1) * tq
        kw = jnp.concatenate([kp_ref[0], km_ref[0], kn_ref[0]], axis=0)
        vwt = jnp.concatenate([vp_ref[0, 0], vm_ref[0, 0], vn_ref[0, 0]], axis=1)
        nk = kw.shape[0]
        kpos = t0 - SWA_WINDOW + lax.broadcasted_iota(jnp.int32, (nk, tq), 0)
        dist = t0 + lax.broadcasted_iota(jnp.int32, (nk, tq), 1) - kpos
        visible = (dist <= SWA_WINDOW) & (dist >= -SWA_WINDOW) & (kpos >= 0) & (kpos < n_total)
        hide = jnp.where(visible, 0.0, NEG)
    sink = sink_ref[...] * LOG2E

    def ext(vt):
        return jnp.concatenate([vt, jnp.ones((ONES_ROWS, vt.shape[1]), BF16)], axis=0)

    qs = [q_ref[0, :, LANE * j:LANE * (j + 1)] for j in range(SWA_HEADS)]
    s_cs = [lax.dot_general(kc, qj, nt, preferred_element_type=F32) for qj in qs]
    s_ws = [lax.dot_general(kw, qj, nt, preferred_element_type=F32) + hide for qj in qs] if has_win else None
    outs = []
    for j in range(SWA_HEADS):
        kv = j // (SWA_HEADS // SWA_KV_HEADS)
        sink_j = sink[:, j:j + 1]
        m = jnp.maximum(jnp.max(s_cs[j], axis=0, keepdims=True), sink_j)
        if has_win:
            m = jnp.maximum(m, jnp.max(s_ws[j], axis=0, keepdims=True))
        acc = jnp.dot(ext(vct[SWA_HEAD * kv:SWA_HEAD * (kv + 1)]), jnp.exp2(s_cs[j] - m).astype(BF16),
                      preferred_element_type=F32)
        if has_win:
            acc = acc + jnp.dot(ext(vwt[SWA_HEAD * kv:SWA_HEAD * (kv + 1)]), jnp.exp2(s_ws[j] - m).astype(BF16),
                                preferred_element_type=F32)
        outs.append(acc[:SWA_HEAD] / (acc[SWA_HEAD:SWA_HEAD + 1] + jnp.exp2(sink_j - m)))
    o_ref[0] = jnp.concatenate(outs, axis=0).T.astype(BF16)


def _swa_call(q, k, vt, k_ctx, vt_ctx, sink, tile):
    b, n, _ = q.shape
    has_win = k is not None
    c = k_ctx.shape[1]
    args, specs = [q], [pl.BlockSpec((1, tile, 4 * LANE), lambda bb, i: (bb, i, 0))]
    if has_win:
        assert vt.shape[3] == tile
        r = tile // SWA_WINDOW
        last = n // SWA_WINDOW - 1
        n_tiles = n // tile
        prev = pl.BlockSpec((1, SWA_WINDOW, LANE), lambda bb, i: (bb, jnp.maximum(i * r - 1, 0), 0))
        main = pl.BlockSpec((1, tile, LANE), lambda bb, i: (bb, i, 0))
        nxt = pl.BlockSpec((1, SWA_WINDOW, LANE), lambda bb, i: (bb, jnp.minimum((i + 1) * r, last), 0))
        vprev = pl.BlockSpec((1, 1, LANE, SWA_WINDOW), lambda bb, i: (bb, jnp.maximum(i - 1, 0), 0, r - 1))
        vmain = pl.BlockSpec((1, 1, LANE, tile), lambda bb, i: (bb, i, 0, 0))
        vnxt = pl.BlockSpec((1, 1, LANE, SWA_WINDOW), lambda bb, i: (bb, jnp.minimum(i + 1, n_tiles - 1), 0, 0))
        args += [k, k, k, vt, vt, vt]
        specs += [prev, main, nxt, vprev, vmain, vnxt]
    args += [k_ctx, vt_ctx, sink]
    specs += [pl.BlockSpec((1, c, LANE), lambda bb, i: (bb, 0, 0)),
              pl.BlockSpec((1, 1, LANE, c), lambda bb, i: (bb, 0, 0, 0)),
              pl.BlockSpec(sink.shape, lambda bb, i: (0, 0))]
    return pl.pallas_call(
        functools.partial(_swa_kernel, has_win=has_win, n_total=n),
        grid=(b, n // tile),
        in_specs=specs,
        out_specs=pl.BlockSpec((1, tile, 2 * LANE), lambda bb, i: (bb, i, 0)),
        out_shape=jax.ShapeDtypeStruct((b, n, 2 * LANE), BF16),
        compiler_params=_cparams(("arbitrary", "arbitrary")),
        name="swa_win" if has_win else "swa_ctx",
    )(*args)


def _pool(u, u_prev, u_next, t0, n_total):
    tile = u.shape[0]
    ext = jnp.concatenate([jnp.where(t0 > 0, u_prev, 0.0), u, jnp.where(t0 + tile < n_total, u_next, 0.0)], axis=0)
    n_ext = ext.shape[0]

    def ahead(a, k):
        return pltpu.roll(a, n_ext - k, 0) if k else a

    sums, run, w = [], ext, 1
    for win in POOL_WINDOWS:
        while w < win:
            run = run + ahead(run, w)
            w *= 2
        sums.append(ahead(run, POOL_HALO - win // 2)[:tile])
    grp = lax.broadcasted_iota(jnp.int32, (tile, BRANCH_W), 1) // POOL_GROUP
    t = t0 + lax.broadcasted_iota(jnp.int32, (tile, BRANCH_W), 0)
    win_sum, half = sums[-1], jnp.full((tile, BRANCH_W), POOL_WINDOWS[-1] // 2, jnp.int32)
    for g in range(len(POOL_WINDOWS) - 2, -1, -1):
        win_sum = jnp.where(grp == g, sums[g], win_sum)
        half = jnp.where(grp == g, POOL_WINDOWS[g] // 2, half)
    lo = jnp.clip(t - half, 0, n_total)
    hi = jnp.clip(t + half, 0, n_total)
    return win_sum / (hi - lo).astype(F32) - u


def _merge_kernel(x_ref, mod_ref, gpre_ref, gpost_ref, wg_ref, wm_ref, up_ref, u_ref, un_ref, wpool_ref, spool_ref,
                  ymla_ref, ydiff_ref, yswa_ref, wbr_ref, wout_ref, o_ref, *, n_total):
    x = x_ref[0]
    d = x.shape[-1]
    tile = x.shape[0]
    mod = mod_ref[0]
    hb = _prenorm(x, mod, gpre_ref[...]).astype(BF16)

    pooled = _pool(u_ref[0], up_ref[0], un_ref[0], pl.program_id(0) * tile, n_total)
    y_pool = jnp.dot(pooled.astype(BF16), wpool_ref[...], preferred_element_type=F32) * spool_ref[...]
    ys = [y_pool, ymla_ref[0].astype(F32), ydiff_ref[0].astype(F32), yswa_ref[0].astype(F32)]

    merged = jnp.zeros((tile, d), F32)
    for r in range(N_BRANCH):
        g = jnp.dot(hb, wg_ref[:, BRANCH_W * r:BRANCH_W * (r + 1)], preferred_element_type=F32)
        g = g * _sigmoid(g)
        mg = _sigmoid(jnp.dot(hb, wm_ref[:, d * r:d * (r + 1)], preferred_element_type=F32))
        merged = merged + mg * jnp.dot((ys[r] * g).astype(BF16), wbr_ref[r], preferred_element_type=F32)
    out = jnp.dot(merged.astype(BF16), wout_ref[...], preferred_element_type=F32)
    o_ref[0] = x + mod[:, 2 * d:] * _rms(out, gpost_ref[...], d)


def _merge_call(x, mod, g_pre, g_post, wg, wm, u, wpool, s_pool, y_mla, y_diff, y_swa, w_br, w_out, tile):
    b, n, d = x.shape
    r = tile // POOL_HALO
    last = n // POOL_HALO - 1
    row = lambda w: pl.BlockSpec((1, tile, w), lambda i, bb: (bb, i, 0))
    return pl.pallas_call(
        functools.partial(_merge_kernel, n_total=n),
        grid=(n // tile, b),
        in_specs=[row(d),
                  pl.BlockSpec((1, 1, 3 * d), lambda i, bb: (bb, 0, 0)),
                  _const_spec(g_pre.shape), _const_spec(g_post.shape), _const_spec(wg.shape), _const_spec(wm.shape),
                  pl.BlockSpec((1, POOL_HALO, BRANCH_W), lambda i, bb: (bb, jnp.maximum(i * r - 1, 0), 0)),
                  row(BRANCH_W),
                  pl.BlockSpec((1, POOL_HALO, BRANCH_W), lambda i, bb: (bb, jnp.minimum((i + 1) * r, last), 0)),
                  _const_spec(wpool.shape), _const_spec(s_pool.shape),
                  row(BRANCH_W), row(BRANCH_W), row(BRANCH_W),
                  _const_spec(w_br.shape), _const_spec(w_out.shape)],
        out_specs=row(d),
        out_shape=jax.ShapeDtypeStruct((b, n, d), F32),
        compiler_params=_cparams(("arbitrary", "arbitrary")),
        name="merge",
    )(x, mod, g_pre, g_post, wg, wm, u, u, u, wpool, s_pool, y_mla, y_diff, y_swa, w_br, w_out)


def _rope_tables(n_tok, identity):
    if identity:
        one, zero = jnp.ones((n_tok, LANE), F32), jnp.zeros((n_tok, LANE), F32)
        return jnp.concatenate([one, zero, one, zero, one, zero], axis=1)
    t = jnp.arange(n_tok, dtype=jnp.int32)
    row = (t // GRID_W).astype(F32)[:, None]
    col = (t % GRID_W).astype(F32)[:, None]

    def cs(rot_dim):
        n_freq = rot_dim // 4
        inv = jnp.exp(-math.log(ROPE_BASE) * jnp.arange(n_freq, dtype=F32) / n_freq)
        ang = jnp.concatenate([row * inv, col * inv], axis=-1)
        c, s = jnp.cos(ang), jnp.sin(ang)
        return jnp.concatenate([c, c], axis=1), jnp.concatenate([-s, s], axis=1)

    c32, s32 = cs(DIFF_QK)
    c64, s64 = cs(SWA_HEAD)
    one, zero = jnp.ones((n_tok, MLA_NOPE), F32), jnp.zeros((n_tok, MLA_NOPE), F32)
    cm = jnp.concatenate([one, c32, one[:, :MLA_ROPE]], axis=1)
    sm = jnp.concatenate([zero, s32, zero[:, :MLA_ROPE]], axis=1)
    return jnp.concatenate([jnp.tile(c32, (1, 4)), jnp.tile(s32, (1, 4)), jnp.tile(c64, (1, 2)),
                            jnp.tile(s64, (1, 2)), cm, sm], axis=1)


def _pack_layer(w_in, w_pool, g_cq, w_uq, w_uk, w_br, w_out):
    d = w_in.shape[0]
    col = lambda name: w_in[:, _OFF[name][0]:_OFF[name][1]]
    z = lambda w: jnp.zeros((d, w), w_in.dtype)
    swa_q = col('swa_q')
    swa_blocks = []
    for j in range(SWA_HEADS):
        qj = swa_q[:, SWA_HEAD * j:SWA_HEAD * (j + 1)]
        swa_blocks += [qj, z(SWA_HEAD)] if j // 2 == 0 else [z(SWA_HEAD), qj]
    wa = jnp.concatenate(
        [col('pool_in'), col('mla_cq'), z(256 - MLA_Q_RANK), col('mla_ckv'),
         z(MLA_NOPE), col('mla_kr'), z(LANE - MLA_NOPE - MLA_ROPE),
         col('diff_q'), col('diff_k'), col('diff_v')] + swa_blocks + [col('swa_k'), col('swa_v')], axis=1)
    hq = MLA_NOPE + MLA_ROPE
    uq = jnp.pad(w_uq.reshape(MLA_Q_RANK, MLA_HEADS, hq), ((0, 256 - MLA_Q_RANK), (0, 0), (0, LANE - hq)))
    uk = jnp.pad(w_uk.reshape(MLA_KV_RANK, MLA_HEADS, MLA_NOPE), ((0, 0), (0, 0), (0, LANE - MLA_NOPE)))
    wpool = jnp.zeros((BRANCH_W, BRANCH_W), w_pool.dtype)
    for g in range(len(POOL_WINDOWS)):
        sl = slice(POOL_GROUP * g, POOL_GROUP * (g + 1))
        wpool = wpool.at[sl, sl].set(w_pool[g])
    return dict(
        wa=wa.astype(BF16), wg=col('gates').astype(BF16), wm=col('merge').astype(BF16),
        g_cq=jnp.pad(g_cq, (0, 256 - MLA_Q_RANK)).reshape(1, 256),
        w_uq=uq.reshape(256, MLA_HEADS * LANE).astype(BF16), w_uk=uk.reshape(MLA_KV_RANK, MLA_HEADS * LANE).astype(BF16),
        wpool=wpool.astype(BF16), w_br=w_br.astype(BF16), w_out=w_out.astype(BF16))


def kernel(x, c, ctx, c_ctx, w_mod, b_mod, g_pre, g_post, w_in, w_pool, s_pool, g_cq, w_uq, g_ckv, w_uk, w_uv,
           lam_q1, lam_k1, lam_q2, lam_k2, g_diff, sink, w_br, w_out):
    b, n, d = x.shape
    n_ctx = ctx.shape[1]
    depth = w_in.shape[0]
    tile = min(SEQ_TILE, n)
    assert n % tile == 0 and n_ctx % LANE == 0 and n_ctx <= SEQ_TILE and d % LANE == 0

    rows = -(-(b + 1) // 8) * 8
    cc = jnp.zeros((rows, d), F32).at[:b].set(c).at[b].set(c_ctx)
    mod_all = _mod_call(cc, w_mod, b_mod)
    tabs = _rope_tables(n, False)
    tabs_ctx = _rope_tables(n_ctx, True)

    xc = ctx
    for l in range(depth):
        need_ctx = l < depth - 1
        lam_init = 0.8 - 0.6 * math.exp(-0.3 * l)
        p = _pack_layer(w_in[l], w_pool[l], g_cq[l], w_uq[l], w_uk[l], w_br[l], w_out[l])
        mod = mod_all[l, :b].reshape(b, 1, 3 * d)
        mod_c = jnp.broadcast_to(mod_all[l, b].reshape(1, 1, 3 * d), (b, 1, 3 * d))
        gp, gpo = g_pre[l].reshape(1, d), g_post[l].reshape(1, d)
        gckv = g_ckv[l].reshape(1, MLA_KV_RANK)
        wuv = w_uv[l].astype(BF16)
        lam_v = jnp.stack([lam_q1[l], lam_k1[l], lam_q2[l], lam_k2[l]])
        gd = g_diff[l].reshape(DIFF_V, 1)
        sk = sink[l].reshape(1, SWA_HEADS)
        sp = s_pool[l].reshape(1, BRANCH_W)

        proj = lambda xx, mm, tt, tl: _proj_call(xx, mm, gp, p['wa'], tt, p['g_cq'], p['w_uq'], gckv, p['w_uk'],
                                                 wuv, tl)
        u, mq, mk, mvt, dq, dk, dvt, sq, skk, sv = proj(x, mod, tabs, tile)
        uc, mqc, mkc, mvtc, dqc, dkc, dvtc, sqc, skc, svc = proj(xc, mod_c, tabs_ctx, n_ctx)

        y_mla = _dense_call("mla", mq, mk, mvt, mkc, mvtc, (), lam_init)
        y_diff = _dense_call("diff", dq, dk, dvt, dkc, dvtc, (lam_v, gd), lam_init)
        y_swa = _swa_call(sq, skk, sv, skc, svc, sk, min(SWA_TILE, n))
        merge = lambda xx, mm, uu, ya, yb, yc, tl: _merge_call(xx, mm, gp, gpo, p['wg'], p['wm'], uu, p['wpool'], sp,
                                                              ya, yb, yc, p['w_br'], p['w_out'], tl)
        x_new = merge(x, mod, u, y_mla, y_diff, y_swa, tile)
        if need_ctx:
            yc_mla = _dense_call("mla", mqc, None, None, mkc, mvtc, (), lam_init)
            yc_diff = _dense_call("diff", dqc, None, None, dkc, dvtc, (lam_v, gd), lam_init)
            yc_swa = _swa_call(sqc, None, None, skc, svc, sk, n_ctx)
            xc = merge(xc, mod_c, uc, yc_mla, yc_diff, yc_swa, n_ctx)
        x = x_new
    return x
```

```python
import functools
import math

import jax
import jax.numpy as jnp
from jax import lax
from jax.experimental import pallas as pl
from jax.experimental.pallas import tpu as pltpu

F32 = jnp.float32
BF16 = jnp.bfloat16

GRID_W = 64
EPS = 1e-6
ROPE_BASE = 10000.0
NEG = -1e30
N_BRANCH = 4
BRANCH_W = 256
POOL_WINDOWS = (2, 4, 8, 16)
POOL_GROUP = BRANCH_W // len(POOL_WINDOWS)
POOL_HALO = 8
MLA_HEADS, MLA_NOPE, MLA_ROPE, MLA_V = 4, 64, 32, 64
MLA_Q_RANK, MLA_KV_RANK = 192, 128
DIFF_HEADS, DIFF_QK, DIFF_V = 4, 32, 64
SWA_HEADS, SWA_KV_HEADS, SWA_HEAD, SWA_WINDOW = 4, 2, 64, 128
HEAD_V = 64
LOG2E = math.log2(math.e)

LANE = 128
ONES_ROWS = 16
VMEM_LIMIT = 56 * 1024 * 1024
SEQ_TILE = 512
SWA_TILE = 512
SWA_QGROUP = 128
EXP2_GUARD = 60.0
KEY_CHUNK = {"mla": 256, "diff": 256}
Q_TILE = {"mla": 512, "diff": 256}

_SPLITS = (('pool_in', 256), ('mla_cq', 192), ('mla_ckv', 128), ('mla_kr', 32), ('diff_q', 256), ('diff_k', 256),
           ('diff_v', 256), ('swa_q', 256), ('swa_k', 128), ('swa_v', 128), ('gates', 1024), ('merge', 4096))
_OFF = {}
_o = 0
for _n, _w in _SPLITS:
    _OFF[_n] = (_o, _o + _w)
    _o += _w

A_POOL, A_CQ, A_CKV, A_KR, A_DQ, A_DK, A_DV, A_SQ, A_SK, A_SV, A_END = (
    0, 256, 512, 640, 768, 1024, 1280, 1536, 2048, 2176, 2304)


def _cparams(sem):
    return pltpu.CompilerParams(dimension_semantics=sem, vmem_limit_bytes=VMEM_LIMIT)


def _const_spec(shape):
    nd = len(shape)
    return pl.BlockSpec(shape, lambda *_: (0,) * nd, pipeline_mode=pl.Buffered(1))


def _sigmoid(x):
    return 0.5 * jnp.tanh(0.5 * x) + 0.5


def _rms(x, g, n):
    return x * lax.rsqrt(jnp.sum(x * x, axis=-1, keepdims=True) * (1.0 / n) + EPS) * g


def _prenorm(x, mod, g_pre):
    d = x.shape[-1]
    shift, scale = mod[:, :d], mod[:, d:2 * d]
    return _rms(x, g_pre, d) * (1.0 + scale) + shift


def _rope128(x, c, s, period, half, first_below):
    lane = lax.broadcasted_iota(jnp.int32, x.shape, 1)
    first = (lane % period) < first_below
    swapped = jnp.where(first, pltpu.roll(x, LANE - half, 1), pltpu.roll(x, half, 1))
    return x * c + swapped * s


def _rope_mla(x, c, s):
    return _rope128(x, c, s, LANE, MLA_ROPE // 2, MLA_NOPE + MLA_ROPE // 2)


def _rope_heads(x, c, s, head_dim):
    return _rope128(x, c, s, head_dim, head_dim // 2, head_dim // 2)


def _mod_kernel(c_ref, w_ref, b_ref, o_ref):
    a = c_ref[...]
    a = a * jax.nn.sigmoid(a)
    w = w_ref[0]
    a_hi = a.astype(BF16)
    a_lo = (a - a_hi.astype(F32)).astype(BF16)
    w_hi = w.astype(BF16)
    w_lo = (w - w_hi.astype(F32)).astype(BF16)
    acc = jnp.dot(a_hi, w_hi, preferred_element_type=F32)
    acc += jnp.dot(a_lo, w_hi, preferred_element_type=F32)
    acc += jnp.dot(a_hi, w_lo, preferred_element_type=F32)
    o_ref[0] = acc + b_ref[0]


def _mod_call(cc, w_mod, b_mod):
    n_layer, d, d3 = w_mod.shape
    rows = cc.shape[0]
    return pl.pallas_call(
        _mod_kernel,
        grid=(n_layer, d3 // d),
        in_specs=[pl.BlockSpec((rows, d), lambda l, j: (0, 0)),
                  pl.BlockSpec((1, d, d), lambda l, j: (l, 0, j)),
                  pl.BlockSpec((1, 1, d), lambda l, j: (l, 0, j))],
        out_specs=pl.BlockSpec((1, rows, d), lambda l, j: (l, 0, j)),
        out_shape=jax.ShapeDtypeStruct((n_layer, rows, d3), F32),
        compiler_params=_cparams(("arbitrary", "arbitrary")),
        name="mod",
    )(cc, w_mod, b_mod.reshape(n_layer, 1, d3))


def _proj_kernel(x_ref, mod_ref, gpre_ref, wa_ref, tabs_ref, gcq_ref, wuq_ref, gckv_ref, wuk_ref, wuv_ref,
                 u_ref, mq_ref, mk_ref, mvt_ref, dq_ref, dk_ref, dvt_ref, sq_ref, sk_ref, svt_ref):
    hb = _prenorm(x_ref[0], mod_ref[0], gpre_ref[...]).astype(BF16)

    def sect(lo, hi):
        return jnp.dot(hb, wa_ref[:, lo:hi], preferred_element_type=F32)

    tabs = tabs_ref[...]
    c32, s32, c64, s64, cm, sm = (tabs[:, LANE * j:LANE * (j + 1)] for j in range(6))

    u_ref[0] = sect(A_POOL, A_CQ)

    cqn = _rms(sect(A_CQ, A_CKV), gcq_ref[...], MLA_Q_RANK).astype(BF16)
    q = jnp.dot(cqn, wuq_ref[...], preferred_element_type=F32)
    mla_scale = (MLA_NOPE + MLA_ROPE) ** -0.5 * LOG2E
    for h in range(MLA_HEADS):
        blk = _rope_mla(q[:, LANE * h:LANE * (h + 1)], cm, sm)
        mq_ref[0, :, LANE * h:LANE * (h + 1)] = (blk * mla_scale).astype(BF16)
    ckvn = _rms(sect(A_CKV, A_KR), gckv_ref[...], MLA_KV_RANK).astype(BF16)
    k_nope = jnp.dot(ckvn, wuk_ref[...], preferred_element_type=F32)
    k_rope = sect(A_KR, A_DQ)
    for h in range(MLA_HEADS):
        blk = _rope_mla(k_nope[:, LANE * h:LANE * (h + 1)] + k_rope, cm, sm)
        mk_ref[0, h] = blk.astype(BF16)
    mvt_ref[0, 0] = jnp.dot(ckvn, wuv_ref[...], preferred_element_type=F32).T.astype(BF16)

    dq = sect(A_DQ, A_DK)
    dk = sect(A_DK, A_DV)
    diff_scale = DIFF_QK ** -0.5 * LOG2E
    for j in range(2):
        sl = slice(LANE * j, LANE * (j + 1))
        dq_ref[0, :, sl] = (_rope_heads(dq[:, sl], c32, s32, DIFF_QK) * diff_scale).astype(BF16)
        dk_ref[0, :, sl] = _rope_heads(dk[:, sl], c32, s32, DIFF_QK).astype(BF16)
    dvt_ref[0, 0] = sect(A_DV, A_SQ).T.astype(BF16)

    sq = sect(A_SQ, A_SK)
    swa_scale = SWA_HEAD ** -0.5 * LOG2E
    for j in range(SWA_HEADS):
        sl = slice(LANE * j, LANE * (j + 1))
        sq_ref[0, :, sl] = (_rope_heads(sq[:, sl], c64, s64, SWA_HEAD) * swa_scale).astype(BF16)
    sk_ref[0] = _rope_heads(sect(A_SK, A_SV), c64, s64, SWA_HEAD).astype(BF16)
    svt_ref[0, 0] = sect(A_SV, A_END).T.astype(BF16)


def _proj_call(x, mod, g_pre, wa, tabs, g_cq, w_uq, g_ckv, w_uk, w_uv, tile):
    b, n, d = x.shape
    nt = n // tile
    row = lambda w: pl.BlockSpec((1, tile, w), lambda i, bb: (bb, i, 0))
    vt = pl.BlockSpec((1, 1, 2 * LANE, tile), lambda i, bb: (bb, i, 0, 0))
    sds = jax.ShapeDtypeStruct
    return pl.pallas_call(
        _proj_kernel,
        grid=(nt, b),
        in_specs=[row(d),
                  pl.BlockSpec((1, 1, 3 * d), lambda i, bb: (bb, 0, 0)),
                  _const_spec(g_pre.shape), _const_spec(wa.shape),
                  pl.BlockSpec((tile, 6 * LANE), lambda i, bb: (i, 0)),
                  _const_spec(g_cq.shape), _const_spec(w_uq.shape), _const_spec(g_ckv.shape),
                  _const_spec(w_uk.shape), _const_spec(w_uv.shape)],
        out_specs=[row(256), row(512), pl.BlockSpec((1, MLA_HEADS, tile, LANE), lambda i, bb: (bb, 0, i, 0)), vt,
                   row(256), row(256), vt, row(512), row(128),
                   pl.BlockSpec((1, 1, LANE, tile), lambda i, bb: (bb, i, 0, 0))],
        out_shape=[sds((b, n, 256), F32), sds((b, n, 512), BF16), sds((b, MLA_HEADS, n, LANE), BF16),
                   sds((b, nt, 256, tile), BF16), sds((b, n, 256), BF16), sds((b, n, 256), BF16),
                   sds((b, nt, 256, tile), BF16), sds((b, n, 512), BF16), sds((b, n, 128), BF16),
                   sds((b, nt, LANE, tile), BF16)],
        compiler_params=_cparams(("arbitrary", "arbitrary")),
        name="proj",
    )(x, mod, g_pre, wa, tabs, g_cq, w_uq, g_ckv, w_uk, w_uv)


def _dense_kernel(*refs, kind, has_lat, lam_init, n_chunks, tk):
    refs = list(refs)
    q_ref = refs.pop(0)
    if has_lat:
        klat_ref, vlat_ref = refs.pop(0), refs.pop(0)
    kctx_ref, vctx_ref = refs.pop(0), refs.pop(0)
    if kind == "diff":
        lam_ref, gd_ref = refs.pop(0), refs.pop(0)
    o_ref = refs.pop(0)
    tq = q_ref.shape[1]
    if has_lat:
        vt_len = vlat_ref.shape[3]

    chains = []
    for j in range(2):
        if kind == "mla":
            chains.append((j, q_ref[0, :, LANE * j:LANE * (j + 1)]))
        else:
            q_all = q_ref[0]
            lane = lax.broadcasted_iota(jnp.int32, q_all.shape, 1)
            for m in range(2):
                lo = 2 * DIFF_QK * j + DIFF_QK * m
                chains.append((j, jnp.where((lane >= lo) & (lane < lo + DIFF_QK), q_all, jnp.zeros_like(q_all))))

    def keys(ref, j, rws):
        return ref[0, j, rws, :] if kind == "mla" else ref[0, rws, :]

    def score(kcs):
        ss = [lax.dot_general(kcs[j], qm, (((1,), (1,)), ((), ())), preferred_element_type=F32) for j, qm in chains]
        return ss, [jnp.max(s, axis=0, keepdims=True) for s in ss]

    def absorb(state, ss, maxes, vtcs):
        new = []
        for (m_run, acc), (j, _), s, mx in zip(state, chains, ss, maxes):
            m_new = jnp.maximum(m_run, mx)
            p = jnp.exp2(s - m_new).astype(BF16)
            new.append((m_new, jnp.exp2(m_run - m_new) * acc + jnp.dot(vtcs[j], p, preferred_element_type=F32)))
        return tuple(new)

    def ext(vt):
        return jnp.concatenate([vt, jnp.ones((ONES_ROWS, vt.shape[1]), BF16)], axis=0)

    def lat_keys(c):
        return [keys(klat_ref, j, slice(c * tk, (c + 1) * tk)) for j in range(2)]

    def lat_vals(c):
        pieces, k = [], c * tk
        while k < (c + 1) * tk:
            pi, off = divmod(k, vt_len)
            n = min((c + 1) * tk - k, vt_len - off)
            pieces.append((k - c * tk, n, pi, off))
            k += n
        return [[(rel, ext(vlat_ref[0, pi, HEAD_V * j:HEAD_V * (j + 1), off:off + n])) for rel, n, pi, off in pieces]
                for j in range(2)]

    def weigh(accs, alphas, ps, c):
        vts = lat_vals(c)
        out = []
        for acc, a, p, (j, _) in zip(accs, alphas, ps, chains):
            acc = acc if a is None else a * acc
            for rel, vt in vts[j]:
                acc = acc + jnp.dot(vt, p[rel:rel + vt.shape[1], :], preferred_element_type=F32)
            out.append(acc)
        return tuple(out)

    def context_state():
        state = tuple((jnp.full((1, tq), NEG, F32), jnp.zeros((HEAD_V + ONES_ROWS, tq), F32)) for _ in chains)
        ss, maxes = score([keys(kctx_ref, j, slice(None)) for j in range(2)])
        return absorb(state, ss, maxes, [ext(vctx_ref[0, 0, HEAD_V * j:HEAD_V * (j + 1), :]) for j in range(2)])

    def exact_path():
        def soften(ms, ss, maxes):
            new_ms, alphas, ps = [], [], []
            for m_run, s, mx in zip(ms, ss, maxes):
                m_new = jnp.maximum(m_run, mx)
                ps.append(jnp.exp2(s - m_new).astype(BF16))
                new_ms.append(m_new)
                alphas.append(jnp.exp2(m_run - m_new))
            return tuple(new_ms), alphas, ps

        scored = {c: score(lat_keys(c)) for c in range(2)}
        state = context_state()
        ms, accs = tuple(m for m, _ in state), tuple(a for _, a in state)
        ms, al_prev, p_prev = soften(ms, *scored.pop(0))
        for c in range(1, n_chunks):
            ms, al_cur, p_cur = soften(ms, *scored.pop(c))
            if c + 1 < n_chunks:
                scored[c + 1] = score(lat_keys(c + 1))
            accs = weigh(accs, al_prev, p_prev, c - 1)
            al_prev, p_prev = al_cur, p_cur
        return weigh(accs, al_prev, p_prev, n_chunks - 1)

    def lagged_path():
        def soften(refs_, ss):
            ps, new_refs, alphas, over = [], [], [], []
            for r, s in zip(refs_, ss):
                mx = jnp.max(s, axis=0, keepdims=True)
                ps.append(jnp.exp2(s - r).astype(BF16))
                r_new = jnp.maximum(r, mx)
                new_refs.append(r_new)
                alphas.append(jnp.exp2(r - r_new))
                over.append(mx - r)
            return ps, new_refs, alphas, over

        def dots(c):
            kcs = lat_keys(c)
            return [lax.dot_general(kcs[j], qm, (((1,), (1,)), ((), ())), preferred_element_type=F32)
                    for j, qm in chains]

        ss = dots(0)
        state = context_state()
        refs_, accs = [m for m, _ in state], tuple(a for _, a in state)
        p_cur, refs_, al_next, excess = soften(refs_, ss)
        al_cur = [None] * len(chains)
        for c in range(n_chunks):
            if c + 1 < n_chunks:
                p_nxt, refs_, al_after, over = soften(refs_, dots(c + 1))
                excess = [jnp.maximum(e, o) for e, o in zip(excess, over)]
            accs = weigh(accs, al_cur, p_cur, c)
            if c + 1 < n_chunks:
                p_cur, al_cur, al_next = p_nxt, al_next, al_after
        worst = excess[0]
        for e in excess[1:]:
            worst = jnp.maximum(worst, e)
        return accs, jnp.max(worst)

    def finish(accs):
        normed = [acc[:HEAD_V] / acc[HEAD_V:HEAD_V + 1] for acc in accs]
        if kind == "mla":
            outs = normed
        else:
            lam_v = lam_ref[...]
            lam = (jnp.exp(jnp.sum(lam_v[0:1] * lam_v[1:2], axis=1, keepdims=True))
                   - jnp.exp(jnp.sum(lam_v[2:3] * lam_v[3:4], axis=1, keepdims=True)) + lam_init)
            outs = []
            for j in range(2):
                o = normed[2 * j] - lam * normed[2 * j + 1]
                o = o * lax.rsqrt(jnp.sum(o * o, axis=0, keepdims=True) * (1.0 / DIFF_V) + EPS)
                outs.append(o * gd_ref[...] * (1.0 - lam_init))
        o_ref[0] = jnp.concatenate(outs, axis=0).T.astype(BF16)

    if not has_lat:
        finish([acc for _, acc in context_state()])
    else:
        accs, worst = lagged_path()
        finish(accs)

        @pl.when(worst > EXP2_GUARD)
        def _():
            finish(exact_path())


def _dense_call(kind, q, k_lat, vt_lat, k_ctx, vt_ctx, extra, lam_init):
    b, n, _ = q.shape
    tq = min(Q_TILE[kind], n)
    assert n % tq == 0
    w = 2 * LANE if kind == "mla" else LANE
    has_lat = k_lat is not None
    c = vt_ctx.shape[3]

    def key_spec(rows):
        if kind == "mla":
            return pl.BlockSpec((1, 2, rows, LANE), lambda bb, hp, i: (bb, hp, 0, 0))
        return pl.BlockSpec((1, rows, LANE), lambda bb, hp, i: (bb, 0, hp))

    args, specs = [q], [pl.BlockSpec((1, tq, w), lambda bb, hp, i: (bb, i, hp))]
    n_chunks = tk = 0
    if has_lat:
        n_lat, vt_len = vt_lat.shape[1] * vt_lat.shape[3], vt_lat.shape[3]
        tk = KEY_CHUNK[kind]
        assert n_lat % tk == 0 and n_lat // tk >= 2
        n_chunks = n_lat // tk
        args += [k_lat, vt_lat]
        specs += [key_spec(n_lat), pl.BlockSpec((1, n_lat // vt_len, LANE, vt_len), lambda bb, hp, i: (bb, 0, hp, 0))]
    args += [k_ctx, vt_ctx]
    specs += [key_spec(c), pl.BlockSpec((1, 1, LANE, c), lambda bb, hp, i: (bb, 0, hp, 0))]
    for e in extra:
        args.append(e)
        specs.append(pl.BlockSpec(e.shape, lambda bb, hp, i: (0, 0)))
    return pl.pallas_call(
        functools.partial(_dense_kernel, kind=kind, has_lat=has_lat, lam_init=lam_init, n_chunks=n_chunks, tk=tk),
        grid=(b, 2, n // tq),
        in_specs=specs,
        out_specs=pl.BlockSpec((1, tq, LANE), lambda bb, hp, i: (bb, i, hp)),
        out_shape=jax.ShapeDtypeStruct((b, n, 2 * LANE), BF16),
        compiler_params=_cparams(("arbitrary", "arbitrary", "arbitrary")),
        name="dense_" + kind + ("_lat" if has_lat else "_ctx"),
    )(*args)


def _swa_kernel(*refs, has_win, n_total):
    refs = list(refs)
    q_ref = refs.pop(0)
    if has_win:
        kp_ref, km_ref, kn_ref, vp_ref, vm_ref, vn_ref = (refs.pop(0) for _ in range(6))
    kc_ref, vc_ref, sink_ref, o_ref = refs
    tq = q_ref.shape[1]
    nt = (((1,), (1,)), ((), ()))
    kc, vct = kc_ref[0], vc_ref[0, 0]
    if has_win:
        t0 = pl.program_id(1) * tq
        kw = jnp.concatenate([kp_ref[0], km_ref[0], kn_ref[0]], axis=0)
        vwt = jnp.concatenate([vp_ref[0, 0], vm_ref[0, 0], vn_ref[0, 0]], axis=1)
        nk = kw.shape[0]
        kpos = t0 - SWA_WINDOW + lax.broadcasted_iota(jnp.int32, (nk, tq), 0)
        dist = t0 + lax.broadcasted_iota(jnp.int32, (nk, tq), 1) - kpos
        visible = (dist <= SWA_WINDOW) & (dist >= -SWA_WINDOW) & (kpos >= 0) & (kpos < n_total)
        hide = jnp.where(visible, 0.0, NEG)
    sink = sink_ref[...] * LOG2E

    def ext(vt):
        return jnp.concatenate([vt, jnp.ones((ONES_ROWS, vt.shape[1]), BF16)], axis=0)

    tg = min(SWA_QGROUP, tq)
    scored = []
    for g in range(tq // tg):
        rows, ksl = slice(tg * g, tg * (g + 1)), slice(tg * g, tg * (g + 1) + 2 * SWA_WINDOW)
        qs = [q_ref[0, rows, LANE * j:LANE * (j + 1)] for j in range(SWA_HEADS)]
        s_cs = [lax.dot_general(kc, qj, nt, preferred_element_type=F32) for qj in qs]
        s_ws = [lax.dot_general(kw[ksl], qj, nt, preferred_element_type=F32) + hide[ksl, rows] for qj in qs] \
            if has_win else None
        scored.append((s_cs, s_ws, ksl))
    cols = []
    for s_cs, s_ws, ksl in scored:
        outs = []
        for j in range(SWA_HEADS):
            kv = j // (SWA_HEADS // SWA_KV_HEADS)
            sink_j = sink[:, j:j + 1]
            m = jnp.maximum(jnp.max(s_cs[j], axis=0, keepdims=True), sink_j)
            if has_win:
                m = jnp.maximum(m, jnp.max(s_ws[j], axis=0, keepdims=True))
            acc = jnp.dot(ext(vct[SWA_HEAD * kv:SWA_HEAD * (kv + 1)]), jnp.exp2(s_cs[j] - m).astype(BF16),
                          preferred_element_type=F32)
            if has_win:
                acc = acc + jnp.dot(ext(vwt[SWA_HEAD * kv:SWA_HEAD * (kv + 1), ksl]),
                                    jnp.exp2(s_ws[j] - m).astype(BF16), preferred_element_type=F32)
            outs.append(acc[:SWA_HEAD] / (acc[SWA_HEAD:SWA_HEAD + 1] + jnp.exp2(sink_j - m)))
        cols.append(jnp.concatenate(outs, axis=0))
    o_ref[0] = jnp.concatenate(cols, axis=1).T.astype(BF16)


def _swa_call(q, k, vt, k_ctx, vt_ctx, sink, tile):
    b, n, _ = q.shape
    has_win = k is not None
    c = k_ctx.shape[1]
    args, specs = [q], [pl.BlockSpec((1, tile, 4 * LANE), lambda bb, i: (bb, i, 0))]
    if has_win:
        assert vt.shape[3] == tile
        r = tile // SWA_WINDOW
        last = n // SWA_WINDOW - 1
        n_tiles = n // tile
        prev = pl.BlockSpec((1, SWA_WINDOW, LANE), lambda bb, i: (bb, jnp.maximum(i * r - 1, 0), 0))
        main = pl.BlockSpec((1, tile, LANE), lambda bb, i: (bb, i, 0))
        nxt = pl.BlockSpec((1, SWA_WINDOW, LANE), lambda bb, i: (bb, jnp.minimum((i + 1) * r, last), 0))
        vprev = pl.BlockSpec((1, 1, LANE, SWA_WINDOW), lambda bb, i: (bb, jnp.maximum(i - 1, 0), 0, r - 1))
        vmain = pl.BlockSpec((1, 1, LANE, tile), lambda bb, i: (bb, i, 0, 0))
        vnxt = pl.BlockSpec((1, 1, LANE, SWA_WINDOW), lambda bb, i: (bb, jnp.minimum(i + 1, n_tiles - 1), 0, 0))
        args += [k, k, k, vt, vt, vt]
        specs += [prev, main, nxt, vprev, vmain, vnxt]
    args += [k_ctx, vt_ctx, sink]
    specs += [pl.BlockSpec((1, c, LANE), lambda bb, i: (bb, 0, 0)),
              pl.BlockSpec((1, 1, LANE, c), lambda bb, i: (bb, 0, 0, 0)),
              pl.BlockSpec(sink.shape, lambda bb, i: (0, 0))]
    return pl.pallas_call(
        functools.partial(_swa_kernel, has_win=has_win, n_total=n),
        grid=(b, n // tile),
        in_specs=specs,
        out_specs=pl.BlockSpec((1, tile, 2 * LANE), lambda bb, i: (bb, i, 0)),
        out_shape=jax.ShapeDtypeStruct((b, n, 2 * LANE), BF16),
        compiler_params=_cparams(("arbitrary", "arbitrary")),
        name="swa_win" if has_win else "swa_ctx",
    )(*args)


def _pool(u, u_prev, u_next, t0, n_total):
    tile = u.shape[0]
    ext = jnp.concatenate([jnp.where(t0 > 0, u_prev, 0.0), u, jnp.where(t0 + tile < n_total, u_next, 0.0)], axis=0)
    n_ext = ext.shape[0]

    def ahead(a, k):
        return pltpu.roll(a, n_ext - k, 0) if k else a

    sums, run, w = [], ext, 1
    for win in POOL_WINDOWS:
        while w < win:
            run = run + ahead(run, w)
            w *= 2
        sums.append(ahead(run, POOL_HALO - win // 2)[:tile])
    grp = lax.broadcasted_iota(jnp.int32, (tile, BRANCH_W), 1) // POOL_GROUP
    t = t0 + lax.broadcasted_iota(jnp.int32, (tile, BRANCH_W), 0)
    win_sum, half = sums[-1], jnp.full((tile, BRANCH_W), POOL_WINDOWS[-1] // 2, jnp.int32)
    for g in range(len(POOL_WINDOWS) - 2, -1, -1):
        win_sum = jnp.where(grp == g, sums[g], win_sum)
        half = jnp.where(grp == g, POOL_WINDOWS[g] // 2, half)
    lo = jnp.clip(t - half, 0, n_total)
    hi = jnp.clip(t + half, 0, n_total)
    return win_sum / (hi - lo).astype(F32) - u


def _merge_kernel(x_ref, mod_ref, gpre_ref, gpost_ref, wg_ref, wm_ref, up_ref, u_ref, un_ref, wpool_ref, spool_ref,
                  ymla_ref, ydiff_ref, yswa_ref, wbr_ref, wout_ref, o_ref, *, n_total):
    x = x_ref[0]
    d = x.shape[-1]
    tile = x.shape[0]
    mod = mod_ref[0]
    hb = _prenorm(x, mod, gpre_ref[...]).astype(BF16)

    pooled = _pool(u_ref[0], up_ref[0], un_ref[0], pl.program_id(0) * tile, n_total)
    y_pool = jnp.dot(pooled.astype(BF16), wpool_ref[...], preferred_element_type=F32) * spool_ref[...]
    ys = [y_pool, ymla_ref[0].astype(F32), ydiff_ref[0].astype(F32), yswa_ref[0].astype(F32)]

    merged = jnp.zeros((tile, d), F32)
    for r in range(N_BRANCH):
        g = jnp.dot(hb, wg_ref[:, BRANCH_W * r:BRANCH_W * (r + 1)], preferred_element_type=F32)
        g = g * _sigmoid(g)
        mg = _sigmoid(jnp.dot(hb, wm_ref[:, d * r:d * (r + 1)], preferred_element_type=F32))
        merged = merged + mg * jnp.dot((ys[r] * g).astype(BF16), wbr_ref[r], preferred_element_type=F32)
    out = jnp.dot(merged.astype(BF16), wout_ref[...], preferred_element_type=F32)
    o_ref[0] = x + mod[:, 2 * d:] * _rms(out, gpost_ref[...], d)


def _merge_call(x, mod, g_pre, g_post, wg, wm, u, wpool, s_pool, y_mla, y_diff, y_swa, w_br, w_out, tile):
    b, n, d = x.shape
    r = tile // POOL_HALO
    last = n // POOL_HALO - 1
    row = lambda w: pl.BlockSpec((1, tile, w), lambda i, bb: (bb, i, 0))
    return pl.pallas_call(
        functools.partial(_merge_kernel, n_total=n),
        grid=(n // tile, b),
        in_specs=[row(d),
                  pl.BlockSpec((1, 1, 3 * d), lambda i, bb: (bb, 0, 0)),
                  _const_spec(g_pre.shape), _const_spec(g_post.shape), _const_spec(wg.shape), _const_spec(wm.shape),
                  pl.BlockSpec((1, POOL_HALO, BRANCH_W), lambda i, bb: (bb, jnp.maximum(i * r - 1, 0), 0)),
                  row(BRANCH_W),
                  pl.BlockSpec((1, POOL_HALO, BRANCH_W), lambda i, bb: (bb, jnp.minimum((i + 1) * r, last), 0)),
                  _const_spec(wpool.shape), _const_spec(s_pool.shape),
                  row(BRANCH_W), row(BRANCH_W), row(BRANCH_W),
                  _const_spec(w_br.shape), _const_spec(w_out.shape)],
        out_specs=row(d),
        out_shape=jax.ShapeDtypeStruct((b, n, d), F32),
        compiler_params=_cparams(("arbitrary", "arbitrary")),
        name="merge",
    )(x, mod, g_pre, g_post, wg, wm, u, u, u, wpool, s_pool, y_mla, y_diff, y_swa, w_br, w_out)


def _rope_tables(n_tok, identity):
    if identity:
        one, zero = jnp.ones((n_tok, LANE), F32), jnp.zeros((n_tok, LANE), F32)
        return jnp.concatenate([one, zero, one, zero, one, zero], axis=1)
    t = jnp.arange(n_tok, dtype=jnp.int32)
    row = (t // GRID_W).astype(F32)[:, None]
    col = (t % GRID_W).astype(F32)[:, None]

    def cs(rot_dim):
        n_freq = rot_dim // 4
        inv = jnp.exp(-math.log(ROPE_BASE) * jnp.arange(n_freq, dtype=F32) / n_freq)
        ang = jnp.concatenate([row * inv, col * inv], axis=-1)
        c, s = jnp.cos(ang), jnp.sin(ang)
        return jnp.concatenate([c, c], axis=1), jnp.concatenate([-s, s], axis=1)

    c32, s32 = cs(DIFF_QK)
    c64, s64 = cs(SWA_HEAD)
    one, zero = jnp.ones((n_tok, MLA_NOPE), F32), jnp.zeros((n_tok, MLA_NOPE), F32)
    cm = jnp.concatenate([one, c32, one[:, :MLA_ROPE]], axis=1)
    sm = jnp.concatenate([zero, s32, zero[:, :MLA_ROPE]], axis=1)
    return jnp.concatenate([jnp.tile(c32, (1, 4)), jnp.tile(s32, (1, 4)), jnp.tile(c64, (1, 2)),
                            jnp.tile(s64, (1, 2)), cm, sm], axis=1)


def _pack_layer(w_in, w_pool, g_cq, w_uq, w_uk, w_br, w_out):
    d = w_in.shape[0]
    col = lambda name: w_in[:, _OFF[name][0]:_OFF[name][1]]
    z = lambda w: jnp.zeros((d, w), w_in.dtype)
    swa_q = col('swa_q')
    swa_blocks = []
    for j in range(SWA_HEADS):
        qj = swa_q[:, SWA_HEAD * j:SWA_HEAD * (j + 1)]
        swa_blocks += [qj, z(SWA_HEAD)] if j // 2 == 0 else [z(SWA_HEAD), qj]
    wa = jnp.concatenate(
        [col('pool_in'), col('mla_cq'), z(256 - MLA_Q_RANK), col('mla_ckv'),
         z(MLA_NOPE), col('mla_kr'), z(LANE - MLA_NOPE - MLA_ROPE),
         col('diff_q'), col('diff_k'), col('diff_v')] + swa_blocks + [col('swa_k'), col('swa_v')], axis=1)
    hq = MLA_NOPE + MLA_ROPE
    uq = jnp.pad(w_uq.reshape(MLA_Q_RANK, MLA_HEADS, hq), ((0, 256 - MLA_Q_RANK), (0, 0), (0, LANE - hq)))
    uk = jnp.pad(w_uk.reshape(MLA_KV_RANK, MLA_HEADS, MLA_NOPE), ((0, 0), (0, 0), (0, LANE - MLA_NOPE)))
    wpool = jnp.zeros((BRANCH_W, BRANCH_W), w_pool.dtype)
    for g in range(len(POOL_WINDOWS)):
        sl = slice(POOL_GROUP * g, POOL_GROUP * (g + 1))
        wpool = wpool.at[sl, sl].set(w_pool[g])
    return dict(
        wa=wa.astype(BF16), wg=col('gates').astype(BF16), wm=col('merge').astype(BF16),
        g_cq=jnp.pad(g_cq, (0, 256 - MLA_Q_RANK)).reshape(1, 256),
        w_uq=uq.reshape(256, MLA_HEADS * LANE).astype(BF16), w_uk=uk.reshape(MLA_KV_RANK, MLA_HEADS * LANE).astype(BF16),
        wpool=wpool.astype(BF16), w_br=w_br.astype(BF16), w_out=w_out.astype(BF16))


def kernel(x, c, ctx, c_ctx, w_mod, b_mod, g_pre, g_post, w_in, w_pool, s_pool, g_cq, w_uq, g_ckv, w_uk, w_uv,
           lam_q1, lam_k1, lam_q2, lam_k2, g_diff, sink, w_br, w_out):
    b, n, d = x.shape
    n_ctx = ctx.shape[1]
    depth = w_in.shape[0]
    tile = min(SEQ_TILE, n)
    assert n % tile == 0 and n_ctx % LANE == 0 and n_ctx <= SEQ_TILE and d % LANE == 0

    rows = -(-(b + 1) // 8) * 8
    cc = jnp.zeros((rows, d), F32).at[:b].set(c).at[b].set(c_ctx)
    mod_all = _mod_call(cc, w_mod, b_mod)
    tabs = _rope_tables(n, False)
    tabs_ctx = _rope_tables(n_ctx, True)

    xc = ctx
    for l in range(depth):
        need_ctx = l < depth - 1
        lam_init = 0.8 - 0.6 * math.exp(-0.3 * l)
        p = _pack_layer(w_in[l], w_pool[l], g_cq[l], w_uq[l], w_uk[l], w_br[l], w_out[l])
        mod = mod_all[l, :b].reshape(b, 1, 3 * d)
        mod_c = jnp.broadcast_to(mod_all[l, b].reshape(1, 1, 3 * d), (b, 1, 3 * d))
        gp, gpo = g_pre[l].reshape(1, d), g_post[l].reshape(1, d)
        gckv = g_ckv[l].reshape(1, MLA_KV_RANK)
        wuv = w_uv[l].astype(BF16)
        lam_v = jnp.stack([lam_q1[l], lam_k1[l], lam_q2[l], lam_k2[l]])
        gd = g_diff[l].reshape(DIFF_V, 1)
        sk = sink[l].reshape(1, SWA_HEADS)
        sp = s_pool[l].reshape(1, BRANCH_W)

        proj = lambda xx, mm, tt, tl: _proj_call(xx, mm, gp, p['wa'], tt, p['g_cq'], p['w_uq'], gckv, p['w_uk'],
                                                 wuv, tl)
        u, mq, mk, mvt, dq, dk, dvt, sq, skk, sv = proj(x, mod, tabs, tile)
        uc, mqc, mkc, mvtc, dqc, dkc, dvtc, sqc, skc, svc = proj(xc, mod_c, tabs_ctx, n_ctx)

        y_mla = _dense_call("mla", mq, mk, mvt, mkc, mvtc, (), lam_init)
        y_diff = _dense_call("diff", dq, dk, dvt, dkc, dvtc, (lam_v, gd), lam_init)
        y_swa = _swa_call(sq, skk, sv, skc, svc, sk, min(SWA_TILE, n))
        merge = lambda xx, mm, uu, ya, yb, yc, tl: _merge_call(xx, mm, gp, gpo, p['wg'], p['wm'], uu, p['wpool'], sp,
                                                              ya, yb, yc, p['w_br'], p['w_out'], tl)
        x_new = merge(x, mod, u, y_mla, y_diff, y_swa, tile)
        if need_ctx:
            yc_mla = _dense_call("mla", mqc, None, None, mkc, mvtc, (), lam_init)
            yc_diff = _dense_call("diff", dqc, None, None, dkc, dvtc, (lam_v, gd), lam_init)
            yc_swa = _swa_call(sqc, None, None, skc, svc, sk, n_ctx)
            xc = merge(xc, mod_c, uc, yc_mla, yc_diff, yc_swa, n_ctx)
        x = x_new
    return x
```

```python
import functools
import math

import jax
import jax.numpy as jnp
from jax import lax
from jax.experimental import pallas as pl
from jax.experimental.pallas import tpu as pltpu

F32 = jnp.float32
BF16 = jnp.bfloat16

GRID_W = 64
EPS = 1e-6
ROPE_BASE = 10000.0
NEG = -1e30
N_BRANCH = 4
BRANCH_W = 256
POOL_WINDOWS = (2, 4, 8, 16)
POOL_GROUP = BRANCH_W // len(POOL_WINDOWS)
POOL_HALO = 8
MLA_HEADS, MLA_NOPE, MLA_ROPE, MLA_V = 4, 64, 32, 64
MLA_Q_RANK, MLA_KV_RANK = 192, 128
DIFF_HEADS, DIFF_QK, DIFF_V = 4, 32, 64
SWA_HEADS, SWA_KV_HEADS, SWA_HEAD, SWA_WINDOW = 4, 2, 64, 128
HEAD_V = 64
LOG2E = math.log2(math.e)

LANE = 128
ONES_ROWS = 16
VMEM_LIMIT = 56 * 1024 * 1024
SEQ_TILE = 512
SWA_TILE = 512
SWA_QGROUP = 128
EXP2_GUARD = 60.0
KEY_CHUNK = {"mla": 256, "diff": 256}
Q_TILE = {"mla": 512, "diff": 256}

_SPLITS = (('pool_in', 256), ('mla_cq', 192), ('mla_ckv', 128), ('mla_kr', 32), ('diff_q', 256), ('diff_k', 256),
           ('diff_v', 256), ('swa_q', 256), ('swa_k', 128), ('swa_v', 128), ('gates', 1024), ('merge', 4096))
_OFF = {}
_o = 0
for _n, _w in _SPLITS:
    _OFF[_n] = (_o, _o + _w)
    _o += _w

A_POOL, A_CQ, A_CKV, A_KR, A_DQ, A_DK, A_DV, A_SQ, A_SK, A_SV, A_END = (
    0, 256, 512, 640, 768, 1024, 1280, 1536, 2048, 2176, 2304)


def _cparams(sem):
    return pltpu.CompilerParams(dimension_semantics=sem, vmem_limit_bytes=VMEM_LIMIT)


def _const_spec(shape):
    nd = len(shape)
    return pl.BlockSpec(shape, lambda *_: (0,) * nd, pipeline_mode=pl.Buffered(1))


def _sigmoid(x):
    return 0.5 * jnp.tanh(0.5 * x) + 0.5


def _rms(x, g, n):
    return x * lax.rsqrt(jnp.sum(x * x, axis=-1, keepdims=True) * (1.0 / n) + EPS) * g


def _prenorm(x, mod, g_pre):
    d = x.shape[-1]
    shift, scale = mod[:, :d], mod[:, d:2 * d]
    return _rms(x, g_pre, d) * (1.0 + scale) + shift


def _rope128(x, c, s, period, half, first_below):
    lane = lax.broadcasted_iota(jnp.int32, x.shape, 1)
    first = (lane % period) < first_below
    swapped = jnp.where(first, pltpu.roll(x, LANE - half, 1), pltpu.roll(x, half, 1))
    return x * c + swapped * s


def _rope_mla(x, c, s):
    return _rope128(x, c, s, LANE, MLA_ROPE // 2, MLA_NOPE + MLA_ROPE // 2)


def _rope_heads(x, c, s, head_dim):
    return _rope128(x, c, s, head_dim, head_dim // 2, head_dim // 2)


def _mod_kernel(c_ref, w_ref, b_ref, o_ref):
    a = c_ref[...]
    a = a * jax.nn.sigmoid(a)
    w = w_ref[0]
    a_hi = a.astype(BF16)
    a_lo = (a - a_hi.astype(F32)).astype(BF16)
    w_hi = w.astype(BF16)
    w_lo = (w - w_hi.astype(F32)).astype(BF16)
    acc = jnp.dot(a_hi, w_hi, preferred_element_type=F32)
    acc += jnp.dot(a_lo, w_hi, preferred_element_type=F32)
    acc += jnp.dot(a_hi, w_lo, preferred_element_type=F32)
    o_ref[0] = acc + b_ref[0]


def _mod_call(cc, w_mod, b_mod):
    n_layer, d, d3 = w_mod.shape
    rows = cc.shape[0]
    return pl.pallas_call(
        _mod_kernel,
        grid=(n_layer, d3 // d),
        in_specs=[pl.BlockSpec((rows, d), lambda l, j: (0, 0)),
                  pl.BlockSpec((1, d, d), lambda l, j: (l, 0, j)),
                  pl.BlockSpec((1, 1, d), lambda l, j: (l, 0, j))],
        out_specs=pl.BlockSpec((1, rows, d), lambda l, j: (l, 0, j)),
        out_shape=jax.ShapeDtypeStruct((n_layer, rows, d3), F32),
        compiler_params=_cparams(("arbitrary", "arbitrary")),
        name="mod",
    )(cc, w_mod, b_mod.reshape(n_layer, 1, d3))


def _proj_kernel(x_ref, mod_ref, gpre_ref, wa_ref, tabs_ref, gcq_ref, wuq_ref, gckv_ref, wuk_ref, wuv_ref,
                 u_ref, mq_ref, mk_ref, mvt_ref, dq_ref, dk_ref, dvt_ref, sq_ref, sk_ref, svt_ref):
    hb = _prenorm(x_ref[0], mod_ref[0], gpre_ref[...]).astype(BF16)

    def sect(lo, hi):
        return jnp.dot(hb, wa_ref[:, lo:hi], preferred_element_type=F32)

    tabs = tabs_ref[...]
    c32, s32, c64, s64, cm, sm = (tabs[:, LANE * j:LANE * (j + 1)] for j in range(6))

    cqn = _rms(sect(A_CQ, A_CKV), gcq_ref[...], MLA_Q_RANK).astype(BF16)
    ckvn = _rms(sect(A_CKV, A_KR), gckv_ref[...], MLA_KV_RANK).astype(BF16)
    u_ref[0] = sect(A_POOL, A_CQ)

    dq = sect(A_DQ, A_DK)
    dk = sect(A_DK, A_DV)
    diff_scale = DIFF_QK ** -0.5 * LOG2E
    for j in range(2):
        sl = slice(LANE * j, LANE * (j + 1))
        dq_ref[0, :, sl] = (_rope_heads(dq[:, sl], c32, s32, DIFF_QK) * diff_scale).astype(BF16)
        dk_ref[0, :, sl] = _rope_heads(dk[:, sl], c32, s32, DIFF_QK).astype(BF16)
    dvt_ref[0, 0] = sect(A_DV, A_SQ).T.astype(BF16)

    sq = sect(A_SQ, A_SK)
    swa_scale = SWA_HEAD ** -0.5 * LOG2E
    for j in range(SWA_HEADS):
        sl = slice(LANE * j, LANE * (j + 1))
        sq_ref[0, :, sl] = (_rope_heads(sq[:, sl], c64, s64, SWA_HEAD) * swa_scale).astype(BF16)
    sk_ref[0] = _rope_heads(sect(A_SK, A_SV), c64, s64, SWA_HEAD).astype(BF16)
    svt_ref[0, 0] = sect(A_SV, A_END).T.astype(BF16)

    k_rope = sect(A_KR, A_DQ)
    q = jnp.dot(cqn, wuq_ref[...], preferred_element_type=F32)
    mla_scale = (MLA_NOPE + MLA_ROPE) ** -0.5 * LOG2E
    for h in range(MLA_HEADS):
        blk = _rope_mla(q[:, LANE * h:LANE * (h + 1)], cm, sm)
        mq_ref[0, :, LANE * h:LANE * (h + 1)] = (blk * mla_scale).astype(BF16)
    k_nope = jnp.dot(ckvn, wuk_ref[...], preferred_element_type=F32)
    for h in range(MLA_HEADS):
        blk = _rope_mla(k_nope[:, LANE * h:LANE * (h + 1)] + k_rope, cm, sm)
        mk_ref[0, h] = blk.astype(BF16)
    mvt_ref[0, 0] = jnp.dot(ckvn, wuv_ref[...], preferred_element_type=F32).T.astype(BF16)


def _proj_call(x, mod, g_pre, wa, tabs, g_cq, w_uq, g_ckv, w_uk, w_uv, tile):
    b, n, d = x.shape
    nt = n // tile
    row = lambda w: pl.BlockSpec((1, tile, w), lambda i, bb: (bb, i, 0))
    vt = pl.BlockSpec((1, 1, 2 * LANE, tile), lambda i, bb: (bb, i, 0, 0))
    sds = jax.ShapeDtypeStruct
    return pl.pallas_call(
        _proj_kernel,
        grid=(nt, b),
        in_specs=[row(d),
                  pl.BlockSpec((1, 1, 3 * d), lambda i, bb: (bb, 0, 0)),
                  _const_spec(g_pre.shape), _const_spec(wa.shape),
                  pl.BlockSpec((tile, 6 * LANE), lambda i, bb: (i, 0)),
                  _const_spec(g_cq.shape), _const_spec(w_uq.shape), _const_spec(g_ckv.shape),
                  _const_spec(w_uk.shape), _const_spec(w_uv.shape)],
        out_specs=[row(256), row(512), pl.BlockSpec((1, MLA_HEADS, tile, LANE), lambda i, bb: (bb, 0, i, 0)), vt,
                   row(256), row(256), vt, row(512), row(128),
                   pl.BlockSpec((1, 1, LANE, tile), lambda i, bb: (bb, i, 0, 0))],
        out_shape=[sds((b, n, 256), F32), sds((b, n, 512), BF16), sds((b, MLA_HEADS, n, LANE), BF16),
                   sds((b, nt, 256, tile), BF16), sds((b, n, 256), BF16), sds((b, n, 256), BF16),
                   sds((b, nt, 256, tile), BF16), sds((b, n, 512), BF16), sds((b, n, 128), BF16),
                   sds((b, nt, LANE, tile), BF16)],
        compiler_params=_cparams(("arbitrary", "arbitrary")),
        name="proj",
    )(x, mod, g_pre, wa, tabs, g_cq, w_uq, g_ckv, w_uk, w_uv)


def _dense_kernel(*refs, kind, has_lat, lam_init, n_chunks, tk):
    refs = list(refs)
    q_ref = refs.pop(0)
    if has_lat:
        klat_ref, vlat_ref = refs.pop(0), refs.pop(0)
    kctx_ref, vctx_ref = refs.pop(0), refs.pop(0)
    if kind == "diff":
        lam_ref, gd_ref = refs.pop(0), refs.pop(0)
    o_ref = refs.pop(0)
    tq = q_ref.shape[1]
    if has_lat:
        vt_len = vlat_ref.shape[3]

    chains = []
    for j in range(2):
        if kind == "mla":
            chains.append((j, q_ref[0, :, LANE * j:LANE * (j + 1)]))
        else:
            q_all = q_ref[0]
            lane = lax.broadcasted_iota(jnp.int32, q_all.shape, 1)
            for m in range(2):
                lo = 2 * DIFF_QK * j + DIFF_QK * m
                chains.append((j, jnp.where((lane >= lo) & (lane < lo + DIFF_QK), q_all, jnp.zeros_like(q_all))))

    def keys(ref, j, rws):
        return ref[0, j, rws, :] if kind == "mla" else ref[0, rws, :]

    def score(kcs):
        ss = [lax.dot_general(kcs[j], qm, (((1,), (1,)), ((), ())), preferred_element_type=F32) for j, qm in chains]
        return ss, [jnp.max(s, axis=0, keepdims=True) for s in ss]

    def absorb(state, ss, maxes, vtcs):
        new = []
        for (m_run, acc), (j, _), s, mx in zip(state, chains, ss, maxes):
            m_new = jnp.maximum(m_run, mx)
            p = jnp.exp2(s - m_new).astype(BF16)
            new.append((m_new, jnp.exp2(m_run - m_new) * acc + jnp.dot(vtcs[j], p, preferred_element_type=F32)))
        return tuple(new)

    def ext(vt):
        return jnp.concatenate([vt, jnp.ones((ONES_ROWS, vt.shape[1]), BF16)], axis=0)

    def lat_keys(c):
        return [keys(klat_ref, j, slice(c * tk, (c + 1) * tk)) for j in range(2)]

    def lat_vals(c):
        pieces, k = [], c * tk
        while k < (c + 1) * tk:
            pi, off = divmod(k, vt_len)
            n = min((c + 1) * tk - k, vt_len - off)
            pieces.append((k - c * tk, n, pi, off))
            k += n
        return [[(rel, ext(vlat_ref[0, pi, HEAD_V * j:HEAD_V * (j + 1), off:off + n])) for rel, n, pi, off in pieces]
                for j in range(2)]

    def weigh(accs, alphas, ps, c):
        vts = lat_vals(c)
        out = []
        for acc, a, p, (j, _) in zip(accs, alphas, ps, chains):
            acc = acc if a is None else a * acc
            for rel, vt in vts[j]:
                acc = acc + jnp.dot(vt, p[rel:rel + vt.shape[1], :], preferred_element_type=F32)
            out.append(acc)
        return tuple(out)

    def context_state():
        state = tuple((jnp.full((1, tq), NEG, F32), jnp.zeros((HEAD_V + ONES_ROWS, tq), F32)) for _ in chains)
        ss, maxes = score([keys(kctx_ref, j, slice(None)) for j in range(2)])
        return absorb(state, ss, maxes, [ext(vctx_ref[0, 0, HEAD_V * j:HEAD_V * (j + 1), :]) for j in range(2)])

    def exact_path():
        def soften(ms, ss, maxes):
            new_ms, alphas, ps = [], [], []
            for m_run, s, mx in zip(ms, ss, maxes):
                m_new = jnp.maximum(m_run, mx)
                ps.append(jnp.exp2(s - m_new).astype(BF16))
                new_ms.append(m_new)
                alphas.append(jnp.exp2(m_run - m_new))
            return tuple(new_ms), alphas, ps

        scored = {c: score(lat_keys(c)) for c in range(2)}
        state = context_state()
        ms, accs = tuple(m for m, _ in state), tuple(a for _, a in state)
        ms, al_prev, p_prev = soften(ms, *scored.pop(0))
        for c in range(1, n_chunks):
            ms, al_cur, p_cur = soften(ms, *scored.pop(c))
            if c + 1 < n_chunks:
                scored[c + 1] = score(lat_keys(c + 1))
            accs = weigh(accs, al_prev, p_prev, c - 1)
            al_prev, p_prev = al_cur, p_cur
        return weigh(accs, al_prev, p_prev, n_chunks - 1)

    def lagged_path():
        def soften(refs_, ss):
            ps, new_refs, alphas, over = [], [], [], []
            for r, s in zip(refs_, ss):
                mx = jnp.max(s, axis=0, keepdims=True)
                ps.append(jnp.exp2(s - r).astype(BF16))
                r_new = jnp.maximum(r, mx)
                new_refs.append(r_new)
                alphas.append(jnp.exp2(r - r_new))
                over.append(mx - r)
            return ps, new_refs, alphas, over

        def dots(c):
            kcs = lat_keys(c)
            return [lax.dot_general(kcs[j], qm, (((1,), (1,)), ((), ())), preferred_element_type=F32)
                    for j, qm in chains]

        ss = dots(0)
        state = context_state()
        refs_, accs = [m for m, _ in state], tuple(a for _, a in state)
        p_cur, refs_, al_next, excess = soften(refs_, ss)
        al_cur = [None] * len(chains)
        for c in range(n_chunks):
            if c + 1 < n_chunks:
                p_nxt, refs_, al_after, over = soften(refs_, dots(c + 1))
                excess = [jnp.maximum(e, o) for e, o in zip(excess, over)]
            accs = weigh(accs, al_cur, p_cur, c)
            if c + 1 < n_chunks:
                p_cur, al_cur, al_next = p_nxt, al_next, al_after
        worst = excess[0]
        for e in excess[1:]:
            worst = jnp.maximum(worst, e)
        return accs, jnp.max(worst)

    def finish(accs):
        normed = [acc[:HEAD_V] / acc[HEAD_V:HEAD_V + 1] for acc in accs]
        if kind == "mla":
            outs = normed
        else:
            lam_v = lam_ref[...]
            lam = (jnp.exp(jnp.sum(lam_v[0:1] * lam_v[1:2], axis=1, keepdims=True))
                   - jnp.exp(jnp.sum(lam_v[2:3] * lam_v[3:4], axis=1, keepdims=True)) + lam_init)
            outs = []
            for j in range(2):
                o = normed[2 * j] - lam * normed[2 * j + 1]
                o = o * lax.rsqrt(jnp.sum(o * o, axis=0, keepdims=True) * (1.0 / DIFF_V) + EPS)
                outs.append(o * gd_ref[...] * (1.0 - lam_init))
        o_ref[0] = jnp.concatenate(outs, axis=0).T.astype(BF16)

    if not has_lat:
        finish([acc for _, acc in context_state()])
    else:
        accs, worst = lagged_path()
        finish(accs)

        @pl.when(worst > EXP2_GUARD)
        def _():
            finish(exact_path())


def _dense_call(kind, q, k_lat, vt_lat, k_ctx, vt_ctx, extra, lam_init):
    b, n, _ = q.shape
    tq = min(Q_TILE[kind], n)
    assert n % tq == 0
    w = 2 * LANE if kind == "mla" else LANE
    has_lat = k_lat is not None
    c = vt_ctx.shape[3]

    def key_spec(rows):
        if kind == "mla":
            return pl.BlockSpec((1, 2, rows, LANE), lambda bb, hp, i: (bb, hp, 0, 0))
        return pl.BlockSpec((1, rows, LANE), lambda bb, hp, i: (bb, 0, hp))

    args, specs = [q], [pl.BlockSpec((1, tq, w), lambda bb, hp, i: (bb, i, hp))]
    n_chunks = tk = 0
    if has_lat:
        n_lat, vt_len = vt_lat.shape[1] * vt_lat.shape[3], vt_lat.shape[3]
        tk = KEY_CHUNK[kind]
        assert n_lat % tk == 0 and n_lat // tk >= 2
        n_chunks = n_lat // tk
        args += [k_lat, vt_lat]
        specs += [key_spec(n_lat), pl.BlockSpec((1, n_lat // vt_len, LANE, vt_len), lambda bb, hp, i: (bb, 0, hp, 0))]
    args += [k_ctx, vt_ctx]
    specs += [key_spec(c), pl.BlockSpec((1, 1, LANE, c), lambda bb, hp, i: (bb, 0, hp, 0))]
    for e in extra:
        args.append(e)
        specs.append(pl.BlockSpec(e.shape, lambda bb, hp, i: (0, 0)))
    return pl.pallas_call(
        functools.partial(_dense_kernel, kind=kind, has_lat=has_lat, lam_init=lam_init, n_chunks=n_chunks, tk=tk),
        grid=(b, 2, n // tq),
        in_specs=specs,
        out_specs=pl.BlockSpec((1, tq, LANE), lambda bb, hp, i: (bb, i, hp)),
        out_shape=jax.ShapeDtypeStruct((b, n, 2 * LANE), BF16),
        compiler_params=_cparams(("arbitrary", "arbitrary", "arbitrary")),
        name="dense_" + kind + ("_lat" if has_lat else "_ctx"),
    )(*args)


def _swa_kernel(*refs, has_win, n_total):
    refs = list(refs)
    q_ref = refs.pop(0)
    if has_win:
        kp_ref, km_ref, kn_ref, vp_ref, vm_ref, vn_ref = (refs.pop(0) for _ in range(6))
    kc_ref, vc_ref, sink_ref, o_ref = refs
    tq = q_ref.shape[1]
    nt = (((1,), (1,)), ((), ()))
    kc, vct = kc_ref[0], vc_ref[0, 0]
    if has_win:
        t0 = pl.program_id(1) * tq
        kw = jnp.concatenate([kp_ref[0], km_ref[0], kn_ref[0]], axis=0)
        vwt = jnp.concatenate([vp_ref[0, 0], vm_ref[0, 0], vn_ref[0, 0]], axis=1)
        nk = kw.shape[0]
        kpos = t0 - SWA_WINDOW + lax.broadcasted_iota(jnp.int32, (nk, tq), 0)
        dist = t0 + lax.broadcasted_iota(jnp.int32, (nk, tq), 1) - kpos
        visible = (dist <= SWA_WINDOW) & (dist >= -SWA_WINDOW) & (kpos >= 0) & (kpos < n_total)
        hide = jnp.where(visible, 0.0, NEG)
    sink = sink_ref[...] * LOG2E

    def ext(vt):
        return jnp.concatenate([vt, jnp.ones((ONES_ROWS, vt.shape[1]), BF16)], axis=0)

    tg = min(SWA_QGROUP, tq)
    scored = []
    for g in range(tq // tg):
        rows, ksl = slice(tg * g, tg * (g + 1)), slice(tg * g, tg * (g + 1) + 2 * SWA_WINDOW)
        qs = [q_ref[0, rows, LANE * j:LANE * (j + 1)] for j in range(SWA_HEADS)]
        s_cs = [lax.dot_general(kc, qj, nt, preferred_element_type=F32) for qj in qs]
        s_ws = [lax.dot_general(kw[ksl], qj, nt, preferred_element_type=F32) + hide[ksl, rows] for qj in qs] \
            if has_win else None
        scored.append((s_cs, s_ws, ksl))
    cols = []
    for s_cs, s_ws, ksl in scored:
        outs = []
        for j in range(SWA_HEADS):
            kv = j // (SWA_HEADS // SWA_KV_HEADS)
            sink_j = sink[:, j:j + 1]
            m = jnp.maximum(jnp.max(s_cs[j], axis=0, keepdims=True), sink_j)
            if has_win:
                m = jnp.maximum(m, jnp.max(s_ws[j], axis=0, keepdims=True))
            acc = jnp.dot(ext(vct[SWA_HEAD * kv:SWA_HEAD * (kv + 1)]), jnp.exp2(s_cs[j] - m).astype(BF16),
                          preferred_element_type=F32)
            if has_win:
                acc = acc + jnp.dot(ext(vwt[SWA_HEAD * kv:SWA_HEAD * (kv + 1), ksl]),
                                    jnp.exp2(s_ws[j] - m).astype(BF16), preferred_element_type=F32)
            outs.append(acc[:SWA_HEAD] / (acc[SWA_HEAD:SWA_HEAD + 1] + jnp.exp2(sink_j - m)))
        cols.append(jnp.concatenate(outs, axis=0))
    o_ref[0] = jnp.concatenate(cols, axis=1).T.astype(BF16)


def _swa_call(q, k, vt, k_ctx, vt_ctx, sink, tile):
    b, n, _ = q.shape
    has_win = k is not None
    c = k_ctx.shape[1]
    args, specs = [q], [pl.BlockSpec((1, tile, 4 * LANE), lambda bb, i: (bb, i, 0))]
    if has_win:
        assert vt.shape[3] == tile
        r = tile // SWA_WINDOW
        last = n // SWA_WINDOW - 1
        n_tiles = n // tile
        prev = pl.BlockSpec((1, SWA_WINDOW, LANE), lambda bb, i: (bb, jnp.maximum(i * r - 1, 0), 0))
        main = pl.BlockSpec((1, tile, LANE), lambda bb, i: (bb, i, 0))
        nxt = pl.BlockSpec((1, SWA_WINDOW, LANE), lambda bb, i: (bb, jnp.minimum((i + 1) * r, last), 0))
        vprev = pl.BlockSpec((1, 1, LANE, SWA_WINDOW), lambda bb, i: (bb, jnp.maximum(i - 1, 0), 0, r - 1))
        vmain = pl.BlockSpec((1, 1, LANE, tile), lambda bb, i: (bb, i, 0, 0))
        vnxt = pl.BlockSpec((1, 1, LANE, SWA_WINDOW), lambda bb, i: (bb, jnp.minimum(i + 1, n_tiles - 1), 0, 0))
        args += [k, k, k, vt, vt, vt]
        specs += [prev, main, nxt, vprev, vmain, vnxt]
    args += [k_ctx, vt_ctx, sink]
    specs += [pl.BlockSpec((1, c, LANE), lambda bb, i: (bb, 0, 0)),
              pl.BlockSpec((1, 1, LANE, c), lambda bb, i: (bb, 0, 0, 0)),
              pl.BlockSpec(sink.shape, lambda bb, i: (0, 0))]
    return pl.pallas_call(
        functools.partial(_swa_kernel, has_win=has_win, n_total=n),
        grid=(b, n // tile),
        in_specs=specs,
        out_specs=pl.BlockSpec((1, tile, 2 * LANE), lambda bb, i: (bb, i, 0)),
        out_shape=jax.ShapeDtypeStruct((b, n, 2 * LANE), BF16),
        compiler_params=_cparams(("arbitrary", "arbitrary")),
        name="swa_win" if has_win else "swa_ctx",
    )(*args)


def _pool(u, u_prev, u_next, t0, n_total):
    tile = u.shape[0]
    ext = jnp.concatenate([jnp.where(t0 > 0, u_prev, 0.0), u, jnp.where(t0 + tile < n_total, u_next, 0.0)], axis=0)
    n_ext = ext.shape[0]

    def ahead(a, k):
        return pltpu.roll(a, n_ext - k, 0) if k else a

    sums, run, w = [], ext, 1
    for win in POOL_WINDOWS:
        while w < win:
            run = run + ahead(run, w)
            w *= 2
        sums.append(ahead(run, POOL_HALO - win // 2)[:tile])
    grp = lax.broadcasted_iota(jnp.int32, (tile, BRANCH_W), 1) // POOL_GROUP
    t = t0 + lax.broadcasted_iota(jnp.int32, (tile, BRANCH_W), 0)
    win_sum, half = sums[-1], jnp.full((tile, BRANCH_W), POOL_WINDOWS[-1] // 2, jnp.int32)
    for g in range(len(POOL_WINDOWS) - 2, -1, -1):
        win_sum = jnp.where(grp == g, sums[g], win_sum)
        half = jnp.where(grp == g, POOL_WINDOWS[g] // 2, half)
    lo = jnp.clip(t - half, 0, n_total)
    hi = jnp.clip(t + half, 0, n_total)
    return win_sum / (hi - lo).astype(F32) - u


def _merge_kernel(x_ref, mod_ref, gpre_ref, gpost_ref, wg_ref, wm_ref, up_ref, u_ref, un_ref, wpool_ref, spool_ref,
                  ymla_ref, ydiff_ref, yswa_ref, wbr_ref, wout_ref, o_ref, *, n_total):
    x = x_ref[0]
    d = x.shape[-1]
    tile = x.shape[0]
    mod = mod_ref[0]
    hb = _prenorm(x, mod, gpre_ref[...]).astype(BF16)

    pooled = _pool(u_ref[0], up_ref[0], un_ref[0], pl.program_id(0) * tile, n_total)
    y_pool = jnp.dot(pooled.astype(BF16), wpool_ref[...], preferred_element_type=F32) * spool_ref[...]
    ys = [y_pool, ymla_ref[0].astype(F32), ydiff_ref[0].astype(F32), yswa_ref[0].astype(F32)]

    merged = jnp.zeros((tile, d), F32)
    for r in range(N_BRANCH):
        g = jnp.dot(hb, wg_ref[:, BRANCH_W * r:BRANCH_W * (r + 1)], preferred_element_type=F32)
        g = g * _sigmoid(g)
        mg = _sigmoid(jnp.dot(hb, wm_ref[:, d * r:d * (r + 1)], preferred_element_type=F32))
        merged = merged + mg * jnp.dot((ys[r] * g).astype(BF16), wbr_ref[r], preferred_element_type=F32)
    out = jnp.dot(merged.astype(BF16), wout_ref[...], preferred_element_type=F32)
    o_ref[0] = x + mod[:, 2 * d:] * _rms(out, gpost_ref[...], d)


def _merge_call(x, mod, g_pre, g_post, wg, wm, u, wpool, s_pool, y_mla, y_diff, y_swa, w_br, w_out, tile):
    b, n, d = x.shape
    r = tile // POOL_HALO
    last = n // POOL_HALO - 1
    row = lambda w: pl.BlockSpec((1, tile, w), lambda i, bb: (bb, i, 0))
    return pl.pallas_call(
        functools.partial(_merge_kernel, n_total=n),
        grid=(n // tile, b),
        in_specs=[row(d),
                  pl.BlockSpec((1, 1, 3 * d), lambda i, bb: (bb, 0, 0)),
                  _const_spec(g_pre.shape), _const_spec(g_post.shape), _const_spec(wg.shape), _const_spec(wm.shape),
                  pl.BlockSpec((1, POOL_HALO, BRANCH_W), lambda i, bb: (bb, jnp.maximum(i * r - 1, 0), 0)),
                  row(BRANCH_W),
                  pl.BlockSpec((1, POOL_HALO, BRANCH_W), lambda i, bb: (bb, jnp.minimum((i + 1) * r, last), 0)),
                  _const_spec(wpool.shape), _const_spec(s_pool.shape),
                  row(BRANCH_W), row(BRANCH_W), row(BRANCH_W),
                  _const_spec(w_br.shape), _const_spec(w_out.shape)],
        out_specs=row(d),
        out_shape=jax.ShapeDtypeStruct((b, n, d), F32),
        compiler_params=_cparams(("arbitrary", "arbitrary")),
        name="merge",
    )(x, mod, g_pre, g_post, wg, wm, u, u, u, wpool, s_pool, y_mla, y_diff, y_swa, w_br, w_out)


def _rope_tables(n_tok, identity):
    if identity:
        one, zero = jnp.ones((n_tok, LANE), F32), jnp.zeros((n_tok, LANE), F32)
        return jnp.concatenate([one, zero, one, zero, one, zero], axis=1)
    t = jnp.arange(n_tok, dtype=jnp.int32)
    row = (t // GRID_W).astype(F32)[:, None]
    col = (t % GRID_W).astype(F32)[:, None]

    def cs(rot_dim):
        n_freq = rot_dim // 4
        inv = jnp.exp(-math.log(ROPE_BASE) * jnp.arange(n_freq, dtype=F32) / n_freq)
        ang = jnp.concatenate([row * inv, col * inv], axis=-1)
        c, s = jnp.cos(ang), jnp.sin(ang)
        return jnp.concatenate([c, c], axis=1), jnp.concatenate([-s, s], axis=1)

    c32, s32 = cs(DIFF_QK)
    c64, s64 = cs(SWA_HEAD)
    one, zero = jnp.ones((n_tok, MLA_NOPE), F32), jnp.zeros((n_tok, MLA_NOPE), F32)
    cm = jnp.concatenate([one, c32, one[:, :MLA_ROPE]], axis=1)
    sm = jnp.concatenate([zero, s32, zero[:, :MLA_ROPE]], axis=1)
    return jnp.concatenate([jnp.tile(c32, (1, 4)), jnp.tile(s32, (1, 4)), jnp.tile(c64, (1, 2)),
                            jnp.tile(s64, (1, 2)), cm, sm], axis=1)


def _pack_layer(w_in, w_pool, g_cq, w_uq, w_uk, w_br, w_out):
    d = w_in.shape[0]
    col = lambda name: w_in[:, _OFF[name][0]:_OFF[name][1]]
    z = lambda w: jnp.zeros((d, w), w_in.dtype)
    swa_q = col('swa_q')
    swa_blocks = []
    for j in range(SWA_HEADS):
        qj = swa_q[:, SWA_HEAD * j:SWA_HEAD * (j + 1)]
        swa_blocks += [qj, z(SWA_HEAD)] if j // 2 == 0 else [z(SWA_HEAD), qj]
    wa = jnp.concatenate(
        [col('pool_in'), col('mla_cq'), z(256 - MLA_Q_RANK), col('mla_ckv'),
         z(MLA_NOPE), col('mla_kr'), z(LANE - MLA_NOPE - MLA_ROPE),
         col('diff_q'), col('diff_k'), col('diff_v')] + swa_blocks + [col('swa_k'), col('swa_v')], axis=1)
    hq = MLA_NOPE + MLA_ROPE
    uq = jnp.pad(w_uq.reshape(MLA_Q_RANK, MLA_HEADS, hq), ((0, 256 - MLA_Q_RANK), (0, 0), (0, LANE - hq)))
    uk = jnp.pad(w_uk.reshape(MLA_KV_RANK, MLA_HEADS, MLA_NOPE), ((0, 0), (0, 0), (0, LANE - MLA_NOPE)))
    wpool = jnp.zeros((BRANCH_W, BRANCH_W), w_pool.dtype)
    for g in range(len(POOL_WINDOWS)):
        sl = slice(POOL_GROUP * g, POOL_GROUP * (g + 1))
        wpool = wpool.at[sl, sl].set(w_pool[g])
    return dict(
        wa=wa.astype(BF16), wg=col('gates').astype(BF16), wm=col('merge').astype(BF16),
        g_cq=jnp.pad(g_cq, (0, 256 - MLA_Q_RANK)).reshape(1, 256),
        w_uq=uq.reshape(256, MLA_HEADS * LANE).astype(BF16), w_uk=uk.reshape(MLA_KV_RANK, MLA_HEADS * LANE).astype(BF16),
        wpool=wpool.astype(BF16), w_br=w_br.astype(BF16), w_out=w_out.astype(BF16))


def kernel(x, c, ctx, c_ctx, w_mod, b_mod, g_pre, g_post, w_in, w_pool, s_pool, g_cq, w_uq, g_ckv, w_uk, w_uv,
           lam_q1, lam_k1, lam_q2, lam_k2, g_diff, sink, w_br, w_out):
    b, n, d = x.shape
    n_ctx = ctx.shape[1]
    depth = w_in.shape[0]
    tile = min(SEQ_TILE, n)
    assert n % tile == 0 and n_ctx % LANE == 0 and n_ctx <= SEQ_TILE and d % LANE == 0

    rows = -(-(b + 1) // 8) * 8
    cc = jnp.zeros((rows, d), F32).at[:b].set(c).at[b].set(c_ctx)
    mod_all = _mod_call(cc, w_mod, b_mod)
    tabs = _rope_tables(n, False)
    tabs_ctx = _rope_tables(n_ctx, True)

    xc = ctx
    for l in range(depth):
        need_ctx = l < depth - 1
        lam_init = 0.8 - 0.6 * math.exp(-0.3 * l)
        p = _pack_layer(w_in[l], w_pool[l], g_cq[l], w_uq[l], w_uk[l], w_br[l], w_out[l])
        mod = mod_all[l, :b].reshape(b, 1, 3 * d)
        mod_c = jnp.broadcast_to(mod_all[l, b].reshape(1, 1, 3 * d), (b, 1, 3 * d))
        gp, gpo = g_pre[l].reshape(1, d), g_post[l].reshape(1, d)
        gckv = g_ckv[l].reshape(1, MLA_KV_RANK)
        wuv = w_uv[l].astype(BF16)
        lam_v = jnp.stack([lam_q1[l], lam_k1[l], lam_q2[l], lam_k2[l]])
        gd = g_diff[l].reshape(DIFF_V, 1)
        sk = sink[l].reshape(1, SWA_HEADS)
        sp = s_pool[l].reshape(1, BRANCH_W)

        proj = lambda xx, mm, tt, tl: _proj_call(xx, mm, gp, p['wa'], tt, p['g_cq'], p['w_uq'], gckv, p['w_uk'],
                                                 wuv, tl)
        u, mq, mk, mvt, dq, dk, dvt, sq, skk, sv = proj(x, mod, tabs, tile)
        uc, mqc, mkc, mvtc, dqc, dkc, dvtc, sqc, skc, svc = proj(xc, mod_c, tabs_ctx, n_ctx)

        y_mla = _dense_call("mla", mq, mk, mvt, mkc, mvtc, (), lam_init)
        y_diff = _dense_call("diff", dq, dk, dvt, dkc, dvtc, (lam_v, gd), lam_init)
        y_swa = _swa_call(sq, skk, sv, skc, svc, sk, min(SWA_TILE, n))
        merge = lambda xx, mm, uu, ya, yb, yc, tl: _merge_call(xx, mm, gp, gpo, p['wg'], p['wm'], uu, p['wpool'], sp,
                                                              ya, yb, yc, p['w_br'], p['w_out'], tl)
        x_new = merge(x, mod, u, y_mla, y_diff, y_swa, tile)
        if need_ctx:
            yc_mla = _dense_call("mla", mqc, None, None, mkc, mvtc, (), lam_init)
            yc_diff = _dense_call("diff", dqc, None, None, dkc, dvtc, (lam_v, gd), lam_init)
            yc_swa = _swa_call(sqc, None, None, skc, svc, sk, n_ctx)
            xc = merge(xc, mod_c, uc, yc_mla, yc_diff, yc_swa, n_ctx)
        x = x_new
    return x
```
